```python
import jax, jax.numpy as jnp
from jax import lax
import numpy as np

D_MODEL = 2048
BATCH = 16
SEQ = 2048
DEPTH = 4

N_MIXERS = 3
N_A = (DEPTH + 2) // 3
N_B = (DEPTH + 1) // 3
N_C = DEPTH // 3
RMS_EPS = 1e-6
FFN_DIM = D_MODEL * 11 // 4
FFN_CONV = 3
A_DIM = D_MODEL
A_CHUNK = 128
A_GROUP_DIM = 128
A_GROUPS = A_DIM // A_GROUP_DIM
B_HEAD_DIM = 128
B_HEADS = D_MODEL // B_HEAD_DIM
B_DIM = B_HEADS * B_HEAD_DIM
B_BLOCK = 128
C_HEAD_DIM = 128
C_K_HEADS = D_MODEL // C_HEAD_DIM
C_V_HEADS = 2 * C_K_HEADS
C_DK = C_K_HEADS * C_HEAD_DIM
C_DV = C_V_HEADS * C_HEAD_DIM
C_CONV = 4
C_CHUNK = 64

kernel_name = 'hybrid_gmlp_fox_gdn_convffn'


def rmsnorm(x, g):
    xf = x.astype(jnp.float32)
    y = xf * lax.rsqrt(jnp.mean(xf * xf, axis=-1, keepdims=True) + RMS_EPS)
    return (y * g.astype(jnp.float32)).astype(x.dtype)


def l2norm(x):
    xf = x.astype(jnp.float32)
    return xf * lax.rsqrt(jnp.sum(xf * xf, axis=-1, keepdims=True) + RMS_EPS)


def causal_dwconv(x, w):
    k, c = w.shape
    return lax.conv_general_dilated(
        x, w.astype(x.dtype)[:, None, :], window_strides=(1,), padding=[(k - 1, 0)],
        dimension_numbers=('NWC', 'WIO', 'NWC'), feature_group_count=c)


def conv_ffn(x, w_gate, w_up, conv_w, conv_b, w_down):
    gate = causal_dwconv(x @ w_gate, conv_w) + conv_b
    return (jax.nn.silu(gate) * (x @ w_up)) @ w_down


def mixer_gmlp(x, w_in, b_in, v_norm, w_s, b_s, w_out):
    bsz, s, _ = x.shape
    h = jax.nn.gelu(x @ w_in + b_in)
    u, v = jnp.split(h, 2, axis=-1)
    v = rmsnorm(v, v_norm).reshape(bsz, s // A_CHUNK, A_CHUNK, A_GROUPS, A_GROUP_DIM)
    tri = jnp.tril(jnp.ones((A_CHUNK, A_CHUNK), dtype=bool))
    w_causal = jnp.where(tri, w_s, 0)
    sv = jnp.einsum('gts,bnsgd->bntgd', w_causal.astype(v.dtype), v) + b_s.T[None, None, :, :, None]
    return (u * sv.reshape(bsz, s, A_DIM)) @ w_out


def mixer_fox(x, w_in, b_f, q_norm, k_norm, w_out):
    bsz, s, _ = x.shape
    proj = x @ w_in
    q, k, v, og, fl = jnp.split(proj, [B_DIM, 2 * B_DIM, 3 * B_DIM, 4 * B_DIM], axis=-1)
    q = rmsnorm(q.reshape(bsz, s, B_HEADS, B_HEAD_DIM), q_norm).transpose(0, 2, 1, 3)
    k = rmsnorm(k.reshape(bsz, s, B_HEADS, B_HEAD_DIM), k_norm).transpose(0, 2, 1, 3)
    v = v.reshape(bsz, s, B_HEADS, B_HEAD_DIM).transpose(0, 2, 1, 3)
    log_f = jax.nn.log_sigmoid((fl + b_f).astype(jnp.float32))
    c = jnp.cumsum(log_f, axis=1).transpose(0, 2, 1)
    scale = B_HEAD_DIM ** -0.5
    tri = jnp.tril(jnp.ones((B_BLOCK, B_BLOCK), dtype=bool))
    outs = []
    for i in range(s // B_BLOCK):
        lo, hi = i * B_BLOCK, (i + 1) * B_BLOCK
        sc = jnp.einsum('bhqd,bhkd->bhqk', q[:, :, lo:hi], k[:, :, :hi]).astype(jnp.float32) * scale
        sc = sc + c[:, :, lo:hi, None] - c[:, :, None, :hi]
        mask = jnp.concatenate([jnp.ones((B_BLOCK, lo), dtype=bool), tri], axis=1)
        sc = jnp.where(mask, sc, -jnp.inf)
        p = jax.nn.softmax(sc, axis=-1).astype(v.dtype)
        outs.append(jnp.einsum('bhqk,bhkd->bhqd', p, v[:, :, :hi]))
    o = jnp.concatenate(outs, axis=2).transpose(0, 2, 1, 3).reshape(bsz, s, B_DIM)
    return (o * jax.nn.sigmoid(og)) @ w_out


def gated_delta_rule_chunked(q, k, v, beta, g):
    bsz, s, h, dk = q.shape
    dv = v.shape[-1]
    n = s // C_CHUNK

    def chunks(t):
        return t.reshape(bsz, n, C_CHUNK, h, -1).transpose(1, 0, 3, 2, 4)

    q, k, v = chunks(q), chunks(k), chunks(v)
    beta = beta.reshape(bsz, n, C_CHUNK, h).transpose(1, 0, 3, 2)
    g = jnp.cumsum(g.reshape(bsz, n, C_CHUNK, h).transpose(1, 0, 3, 2), axis=-1)
    tri = jnp.tril(jnp.ones((C_CHUNK, C_CHUNK), dtype=bool))
    strict = jnp.tril(jnp.ones((C_CHUNK, C_CHUNK), dtype=bool), -1)
    decay = jnp.exp(jnp.where(tri, g[..., :, None] - g[..., None, :], -jnp.inf))
    kb = k * beta[..., None]
    a_mat = jnp.where(strict, jnp.einsum('nbhtd,nbhsd->nbhts', kb, k) * decay, 0.0)
    rhs = jnp.concatenate([v * beta[..., None], kb * jnp.exp(g)[..., None]], axis=-1)
    sol = lax.linalg.triangular_solve(a_mat + jnp.eye(C_CHUNK, dtype=a_mat.dtype), rhs,
                                      left_side=True, lower=True, unit_diagonal=True)
    u, w = sol[..., :dv], sol[..., dv:]
    attn = jnp.einsum('nbhtd,nbhsd->nbhts', q, k) * decay
    q_dec = q * jnp.exp(g)[..., None]
    g_last = g[..., -1]
    k_dec = k * jnp.exp(g_last[..., None] - g)[..., None]

    def step(state, xs):
        u_n, w_n, attn_n, q_n, k_n, gl = xs
        v_new = u_n - jnp.einsum('bhcd,bhde->bhce', w_n, state)
        o = jnp.einsum('bhcd,bhde->bhce', q_n, state) + jnp.einsum('bhts,bhse->bhte', attn_n, v_new)
        state = state * jnp.exp(gl)[..., None, None] + jnp.einsum('bhcd,bhce->bhde', k_n, v_new)
        return state, o

    state0 = jnp.zeros((bsz, h, dk, dv), dtype=jnp.float32)
    _, o = lax.scan(step, state0, (u, w, attn, q_dec, k_dec, g_last))
    return o.transpose(1, 0, 3, 2, 4).reshape(bsz, s, h, dv)


def mixer_gdn(x, w_in, conv_w, a_log, dt_bias, out_norm, w_out):
    bsz, s, _ = x.shape
    proj = x @ w_in
    qkv, z, b, a = jnp.split(proj, [2 * C_DK + C_DV, 2 * C_DK + 2 * C_DV, 2 * C_DK + 2 * C_DV + C_V_HEADS], axis=-1)
    qkv = jax.nn.silu(causal_dwconv(qkv, conv_w))
    q, k, v = jnp.split(qkv, [C_DK, 2 * C_DK], axis=-1)
    rep = C_V_HEADS // C_K_HEADS
    q = jnp.repeat(l2norm(q.reshape(bsz, s, C_K_HEADS, C_HEAD_DIM)), rep, axis=2) * (C_HEAD_DIM ** -0.5)
    k = jnp.repeat(l2norm(k.reshape(bsz, s, C_K_HEADS, C_HEAD_DIM)), rep, axis=2)
    v = v.reshape(bsz, s, C_V_HEADS, C_HEAD_DIM).astype(jnp.float32)
    beta = jax.nn.sigmoid(b.astype(jnp.float32))
    g = -jnp.exp(a_log.astype(jnp.float32)) * jax.nn.softplus(a.astype(jnp.float32) + dt_bias.astype(jnp.float32))
    o = gated_delta_rule_chunked(q, k, v, beta, g)
    o = rmsnorm(o, out_norm) * jax.nn.silu(z.reshape(bsz, s, C_V_HEADS, C_HEAD_DIM).astype(jnp.float32))
    return o.reshape(bsz, s, C_DV).astype(x.dtype) @ w_out


def setup_inputs(seed: int = 0) -> dict:
    key = jax.random.key(seed)
    keys = list(jax.random.split(key, 32))
    ctr = [0]

    def nk():
        ctr[0] += 1
        return keys[ctr[0] - 1]

    def nrm(shape, scale):
        return scale * jax.random.normal(nk(), shape, jnp.float32)

    def gain(shape):
        return 1.0 + 0.05 * jax.random.normal(nk(), shape, jnp.float32)

    def unif(shape, lo, hi):
        return jax.random.uniform(nk(), shape, jnp.float32, lo, hi)

    d = D_MODEL
    dt = jnp.exp(unif((N_C, C_V_HEADS), float(np.log(1e-3)), float(np.log(1e-1))))
    return {
        'x': nrm((BATCH, SEQ, d), 1.0),
        'norm_mix': gain((DEPTH, d)),
        'norm_ffn': gain((DEPTH, d)),
        'ffn_w_gate': nrm((DEPTH, d, FFN_DIM), d ** -0.5),
        'ffn_w_up': nrm((DEPTH, d, FFN_DIM), d ** -0.5),
        'ffn_conv_w': nrm((DEPTH, FFN_CONV, FFN_DIM), FFN_CONV ** -0.5),
        'ffn_conv_b': nrm((DEPTH, FFN_DIM), 0.02),
        'ffn_w_down': nrm((DEPTH, FFN_DIM, d), FFN_DIM ** -0.5),
        'a_w_in': nrm((N_A, d, 2 * A_DIM), d ** -0.5),
        'a_b_in': nrm((N_A, 2 * A_DIM), 0.02),
        'a_v_norm': gain((N_A, A_DIM)),
        'a_w_s': nrm((N_A, A_GROUPS, A_CHUNK, A_CHUNK), A_CHUNK ** -0.5),
        'a_b_s': gain((N_A, A_GROUPS, A_CHUNK)),
        'a_w_out': nrm((N_A, A_DIM, d), A_DIM ** -0.5),
        'b_w_in': nrm((N_B, d, 4 * B_DIM + B_HEADS), d ** -0.5),
        'b_b_f': unif((N_B, B_HEADS), 1.0, 4.0),
        'b_q_norm': gain((N_B, B_HEAD_DIM)),
        'b_k_norm': gain((N_B, B_HEAD_DIM)),
        'b_w_out': nrm((N_B, B_DIM, d), B_DIM ** -0.5),
        'c_w_in': nrm((N_C, d, 2 * C_DK + 2 * C_DV + 2 * C_V_HEADS), d ** -0.5),
        'c_conv_w': nrm((N_C, C_CONV, 2 * C_DK + C_DV), C_CONV ** -0.5),
        'c_a_log': jnp.log(unif((N_C, C_V_HEADS), 1.0, 16.0)),
        'c_dt_bias': dt + jnp.log(-jnp.expm1(-dt)),
        'c_out_norm': gain((N_C, C_HEAD_DIM)),
        'c_w_out': nrm((N_C, C_DV, d), C_DV ** -0.5),
    }


def reference(x, norm_mix, norm_ffn, ffn_w_gate, ffn_w_up, ffn_conv_w, ffn_conv_b, ffn_w_down,
              a_w_in, a_b_in, a_v_norm, a_w_s, a_b_s, a_w_out,
              b_w_in, b_b_f, b_q_norm, b_k_norm, b_w_out,
              c_w_in, c_conv_w, c_a_log, c_dt_bias, c_out_norm, c_w_out):
    for i in range(DEPTH):
        kind, j = i % N_MIXERS, i // N_MIXERS
        h = rmsnorm(x, norm_mix[i])
        if kind == 0:
            m = mixer_gmlp(h, a_w_in[j], a_b_in[j], a_v_norm[j], a_w_s[j], a_b_s[j], a_w_out[j])
        elif kind == 1:
            m = mixer_fox(h, b_w_in[j], b_b_f[j], b_q_norm[j], b_k_norm[j], b_w_out[j])
        else:
            m = mixer_gdn(h, c_w_in[j], c_conv_w[j], c_a_log[j], c_dt_bias[j], c_out_norm[j], c_w_out[j])
        x = x + m
        x = x + conv_ffn(rmsnorm(x, norm_ffn[i]), ffn_w_gate[i], ffn_w_up[i], ffn_conv_w[i],
                         ffn_conv_b[i], ffn_w_down[i])
    return x
```

```python
import functools

import jax
import jax.numpy as jnp
from jax import lax
from jax.experimental import pallas as pl
from jax.experimental.pallas import tpu as pltpu

RMS_EPS = 1e-6
HEAD_DIM = 128
GMLP_CHUNK = 128
GDN_CHUNK = 128
LANES = 128
HALO = 16
GATE_LANE = 32

TILE_M = 512
TILE_N = 512
ATTN_TILE = 512
GMLP_TILE_M = 256
VMEM_LIMIT_BYTES = 56 * 1024 * 1024

F32 = jnp.float32
BF16 = jnp.bfloat16


def _params(*semantics):
    return pltpu.CompilerParams(dimension_semantics=semantics, vmem_limit_bytes=VMEM_LIMIT_BYTES)


def _dot(a, b):
    return jnp.dot(a, b, preferred_element_type=F32)


def _dot_nt(a, b):
    return lax.dot_general(a, b, (((1,), (1,)), ((), ())), preferred_element_type=F32)


def _dot_tn(a, b):
    return lax.dot_general(a, b, (((0,), (0,)), ((), ())), preferred_element_type=F32)


def _rms_rows(xf, gain):
    ms = jnp.mean(xf * xf, axis=-1, keepdims=True)
    return xf * lax.rsqrt(ms + RMS_EPS) * gain


def _sigmoid(x):
    return 1.0 / (1.0 + jnp.exp(-x))


def _silu(x):
    return x * _sigmoid(x)


def _softplus(x):
    return jnp.maximum(x, 0.0) + jnp.log(1.0 + jnp.exp(-jnp.abs(x)))


def _gelu_tanh(x):
    return 0.5 * x * (1.0 + jnp.tanh(0.7978845608028654 * (x + 0.044715 * (x * x * x))))


def _split3(x):
    hi = x.astype(BF16)
    r1 = x - hi.astype(F32)
    mid = r1.astype(BF16)
    lo = (r1 - mid.astype(F32)).astype(BF16)
    return hi, mid, lo


def _lane_select(block, idx):
    lane = lax.broadcasted_iota(jnp.int32, block.shape, 1)
    return jnp.sum(jnp.where(lane == idx, block, 0.0), axis=1, keepdims=True)


def _causal_conv_rows(acc, prev, w_ref):
    taps = w_ref.shape[0]
    out = acc * w_ref[taps - 1:taps, :]
    top = jnp.concatenate([prev[HALO - 8:, :], acc[:8, :]], axis=0)
    for s in range(1, taps):
        rolled = pltpu.roll(acc, s, 0)
        rolled_top = pltpu.roll(top, s, 0)[8:16, :]
        shifted = jnp.concatenate([rolled_top, rolled[8:, :]], axis=0)
        out = out + shifted * w_ref[taps - 1 - s:taps - s, :]
    return out


def _store_normed(x_ref, g_ref, xn_ref):
    xn_ref[...] = _rms_rows(x_ref[...], g_ref[...]).astype(BF16)


def _gmlp_in_kernel(x_ref, g_ref, w_ref, b_ref, o_ref, xn_ref):
    @pl.when(pl.program_id(1) == 0)
    def _():
        _store_normed(x_ref, g_ref, xn_ref)

    acc = _dot(xn_ref[...], w_ref[...]) + b_ref[...]
    o_ref[...] = _gelu_tanh(acc).astype(o_ref.dtype)


def _fox_in_kernel(x_ref, g_ref, w_ref, gain_ref, w2_ref, o_ref, o2_ref, xn_ref, *, n_norm_tiles):
    j = pl.program_id(1)

    @pl.when(j == 0)
    def _():
        _store_normed(x_ref, g_ref, xn_ref)
        o2_ref[...] = _dot(xn_ref[...], w2_ref[...])

    acc = _dot(xn_ref[...], w_ref[...])

    @pl.when(j < n_norm_tiles)
    def _():
        for c in range(acc.shape[1] // HEAD_DIM):
            sl = slice(c * HEAD_DIM, (c + 1) * HEAD_DIM)
            o_ref[:, sl] = _rms_rows(acc[:, sl], gain_ref[:, sl]).astype(o_ref.dtype)

    @pl.when(j >= n_norm_tiles)
    def _():
        o_ref[...] = acc.astype(o_ref.dtype)


def _seq_start_mask(prev, tiles_per_seq):
    first = (pl.program_id(0) % tiles_per_seq) == 0
    return jnp.where(first, 0.0, prev)


def _gdn_in_kernel(x_ref, xp_ref, g_ref, w_ref, cw_ref, scale_ref, w2_ref, o_ref, o2_ref, xn_ref, xnp_ref,
                   *, n_qk_tiles, n_conv_tiles, tiles_per_seq):
    j = pl.program_id(1)

    @pl.when(j == 0)
    def _():
        _store_normed(x_ref, g_ref, xn_ref)
        _store_normed(xp_ref, g_ref, xnp_ref)
        o2_ref[...] = _dot(xn_ref[...], w2_ref[...])

    acc = _dot(xn_ref[...], w_ref[...])

    def conv_silu():
        prev = _seq_start_mask(_dot(xnp_ref[...], w_ref[...]), tiles_per_seq)
        return _silu(_causal_conv_rows(acc, prev, cw_ref))

    @pl.when(j < n_qk_tiles)
    def _():
        y = conv_silu()
        for c in range(y.shape[1] // HEAD_DIM):
            sl = slice(c * HEAD_DIM, (c + 1) * HEAD_DIM)
            blk = y[:, sl]
            inv = lax.rsqrt(jnp.sum(blk * blk, axis=-1, keepdims=True) + RMS_EPS)
            o_ref[:, sl] = (blk * inv * scale_ref[:, sl]).astype(o_ref.dtype)

    @pl.when(jnp.logical_and(j >= n_qk_tiles, j < n_conv_tiles))
    def _():
        o_ref[...] = conv_silu().astype(o_ref.dtype)

    @pl.when(j >= n_conv_tiles)
    def _():
        o_ref[...] = acc.astype(o_ref.dtype)


def _ffn_in_kernel(x_ref, xp_ref, g_ref, wg_ref, wu_ref, cw_ref, cb_ref, o_ref, xn_ref, xnp_ref, *, tiles_per_seq):
    @pl.when(pl.program_id(1) == 0)
    def _():
        _store_normed(x_ref, g_ref, xn_ref)
        _store_normed(xp_ref, g_ref, xnp_ref)

    xn = xn_ref[...]
    gate = _dot(xn, wg_ref[...])
    up = _dot(xn, wu_ref[...])
    prev = _seq_start_mask(_dot(xnp_ref[...], wg_ref[...]), tiles_per_seq)
    gate = _causal_conv_rows(gate, prev, cw_ref) + cb_ref[...]
    o_ref[...] = (_silu(gate) * up).astype(o_ref.dtype)


def _row_tile(t):
    return min(TILE_M, t)


def _x_specs(tm, d, halo):
    specs = [pl.BlockSpec((tm, d), lambda i, j: (i, 0))]
    if halo:
        per = tm // HALO
        specs.append(pl.BlockSpec((HALO, d), lambda i, j: (jnp.maximum(i * per - 1, 0), 0)))
    return specs


def _col(tn):
    return pl.BlockSpec((1, tn), lambda i, j: (0, j))


def _gmlp_in(x, gain, w, b):
    t, d = x.shape
    n = w.shape[1]
    tm, tn = _row_tile(t), min(TILE_N, n)
    return pl.pallas_call(
        _gmlp_in_kernel,
        grid=(t // tm, n // tn),
        in_specs=_x_specs(tm, d, False) + [pl.BlockSpec((1, d), lambda i, j: (0, 0)),
                                           pl.BlockSpec((d, tn), lambda i, j: (0, j)), _col(tn)],
        out_specs=pl.BlockSpec((tm, tn), lambda i, j: (i, j)),
        out_shape=jax.ShapeDtypeStruct((t, n), BF16),
        scratch_shapes=[pltpu.VMEM((tm, d), BF16)],
        compiler_params=_params("parallel", "arbitrary"),
        name="gmlp_in",
    )(x, gain, w, b)


def _fox_in(x, gain, w, head_gain, w2, n_norm_cols):
    t, d = x.shape
    n = w.shape[1]
    tm, tn = _row_tile(t), min(TILE_N, n_norm_cols)
    kern = functools.partial(_fox_in_kernel, n_norm_tiles=n_norm_cols // tn)
    return pl.pallas_call(
        kern,
        grid=(t // tm, n // tn),
        in_specs=_x_specs(tm, d, False) + [pl.BlockSpec((1, d), lambda i, j: (0, 0)),
                                           pl.BlockSpec((d, tn), lambda i, j: (0, j)), _col(tn),
                                           pl.BlockSpec((d, LANES), lambda i, j: (0, 0))],
        out_specs=[pl.BlockSpec((tm, tn), lambda i, j: (i, j)), pl.BlockSpec((tm, LANES), lambda i, j: (i, 0))],
        out_shape=[jax.ShapeDtypeStruct((t, n), BF16), jax.ShapeDtypeStruct((t, LANES), F32)],
        scratch_shapes=[pltpu.VMEM((tm, d), BF16)],
        compiler_params=_params("parallel", "arbitrary"),
        name="fox_in",
    )(x, gain, w, head_gain, w2)


def _gdn_in(x, gain, w, conv_w, scale_row, w2, n_qk_cols, n_conv_cols, seq):
    t, d = x.shape
    n = w.shape[1]
    tm, tn = _row_tile(min(t, seq)), min(TILE_N, n_qk_cols)
    kern = functools.partial(_gdn_in_kernel, n_qk_tiles=n_qk_cols // tn, n_conv_tiles=n_conv_cols // tn,
                             tiles_per_seq=seq // tm)
    return pl.pallas_call(
        kern,
        grid=(t // tm, n // tn),
        in_specs=_x_specs(tm, d, True) + [pl.BlockSpec((1, d), lambda i, j: (0, 0)),
                                          pl.BlockSpec((d, tn), lambda i, j: (0, j)),
                                          pl.BlockSpec((conv_w.shape[0], tn), lambda i, j: (0, j)), _col(tn),
                                          pl.BlockSpec((d, LANES), lambda i, j: (0, 0))],
        out_specs=[pl.BlockSpec((tm, tn), lambda i, j: (i, j)), pl.BlockSpec((tm, LANES), lambda i, j: (i, 0))],
        out_shape=[jax.ShapeDtypeStruct((t, n), BF16), jax.ShapeDtypeStruct((t, LANES), F32)],
        scratch_shapes=[pltpu.VMEM((tm, d), BF16), pltpu.VMEM((HALO, d), BF16)],
        compiler_params=_params("parallel", "arbitrary"),
        name="gdn_in",
    )(x, x, gain, w, conv_w, scale_row, w2)


def _ffn_in(x, gain, wg, wu, conv_w, conv_b, seq):
    t, d = x.shape
    n = wg.shape[1]
    tm, tn = _row_tile(min(t, seq)), min(TILE_N, n)
    while n % tn:
        tn -= LANES
    kern = functools.partial(_ffn_in_kernel, tiles_per_seq=seq // tm)
    return pl.pallas_call(
        kern,
        grid=(t // tm, n // tn),
        in_specs=_x_specs(tm, d, True) + [pl.BlockSpec((1, d), lambda i, j: (0, 0)),
                                          pl.BlockSpec((d, tn), lambda i, j: (0, j)),
                                          pl.BlockSpec((d, tn), lambda i, j: (0, j)),
                                          pl.BlockSpec((conv_w.shape[0], tn), lambda i, j: (0, j)), _col(tn)],
        out_specs=pl.BlockSpec((tm, tn), lambda i, j: (i, j)),
        out_shape=jax.ShapeDtypeStruct((t, n), BF16),
        scratch_shapes=[pltpu.VMEM((tm, d), BF16), pltpu.VMEM((HALO, d), BF16)],
        compiler_params=_params("parallel", "arbitrary"),
        name="ffn_in",
    )(x, x, gain, wg, wu, conv_w, conv_b)


def _out_proj_kernel(a_ref, w_ref, x_ref, o_ref):
    o_ref[...] = x_ref[...] + _dot(a_ref[...], w_ref[...])


def _out_proj(a, w, x):
    t, k = a.shape
    d = w.shape[1]
    tm, tn = _row_tile(t), min(TILE_N, d)
    return pl.pallas_call(
        _out_proj_kernel,
        grid=(t // tm, d // tn),
        in_specs=[pl.BlockSpec((tm, k), lambda i, j: (i, 0)), pl.BlockSpec((k, tn), lambda i, j: (0, j)),
                  pl.BlockSpec((tm, tn), lambda i, j: (i, j))],
        out_specs=pl.BlockSpec((tm, tn), lambda i, j: (i, j)),
        out_shape=jax.ShapeDtypeStruct((t, d), F32),
        compiler_params=_params("parallel", "arbitrary"),
        name="out_proj",
    )(a, w, x)


def _gmlp_gate_kernel(u_ref, v_ref, vg_ref, ws_ref, bst_ref, o_ref, vn_ref):
    vn_ref[...] = _rms_rows(v_ref[...].astype(F32), vg_ref[...]).astype(BF16)
    tm = u_ref.shape[0]
    groups = ws_ref.shape[0]
    row = lax.broadcasted_iota(jnp.int32, (GMLP_CHUNK, GMLP_CHUNK), 0)
    col = lax.broadcasted_iota(jnp.int32, (GMLP_CHUNK, GMLP_CHUNK), 1)
    for g in range(groups):
        cs = slice(g * HEAD_DIM, (g + 1) * HEAD_DIM)
        w_causal = jnp.where(row >= col, ws_ref[g], 0.0).astype(BF16)
        bias = bst_ref[:, g:g + 1]
        for c in range(tm // GMLP_CHUNK):
            rs = slice(c * GMLP_CHUNK, (c + 1) * GMLP_CHUNK)
            sv = _dot(w_causal, vn_ref[rs, cs]) + bias
            o_ref[rs, cs] = (u_ref[rs, cs].astype(F32) * sv).astype(o_ref.dtype)


def _gmlp_gate(h, v_gain, w_s, b_s_t):
    t, two_a = h.shape
    a = two_a // 2
    tm = min(GMLP_TILE_M, t)
    groups = w_s.shape[0]
    return pl.pallas_call(
        _gmlp_gate_kernel,
        grid=(t // tm,),
        in_specs=[pl.BlockSpec((tm, a), lambda i: (i, 0)), pl.BlockSpec((tm, a), lambda i: (i, 1)),
                  pl.BlockSpec((1, a), lambda i: (0, 0)),
                  pl.BlockSpec((groups, GMLP_CHUNK, GMLP_CHUNK), lambda i: (0, 0, 0)),
                  pl.BlockSpec((GMLP_CHUNK, groups), lambda i: (0, 0))],
        out_specs=pl.BlockSpec((tm, a), lambda i: (i, 0)),
        out_shape=jax.ShapeDtypeStruct((t, a), BF16),
        scratch_shapes=[pltpu.VMEM((tm, a), BF16)],
        compiler_params=_params("parallel"),
        name="gmlp_gate",
    )(h, h, v_gain, w_s, b_s_t)


def _tri_ones(n):
    row = lax.broadcasted_iota(jnp.int32, (n, n), 0)
    col = lax.broadcasted_iota(jnp.int32, (n, n), 1)
    return jnp.where(row >= col, 1.0, 0.0).astype(BF16)


def _cumsum_rows(tri, x):
    hi, mid, lo = _split3(x)
    return _dot(tri, hi) + _dot(tri, mid) + _dot(tri, lo)


def _fox_decay_kernel(fl_ref, bf_ref, ccol_ref, crow_ref):
    tk = crow_ref.shape[2]
    tri = _tri_ones(tk)
    carry = jnp.zeros((1, LANES), F32)
    for n in range(crow_ref.shape[0]):
        rs = slice(n * tk, (n + 1) * tk)
        log_f = -_softplus(-(fl_ref[rs, :] + bf_ref[...]))
        c = _cumsum_rows(tri, log_f) + carry
        carry = c[tk - 1:tk, :]
        ccol_ref[rs, :] = c
        crow_ref[n] = c.T


def _fox_decay(fl, b_f_row, batch, seq, tk):
    nkv = seq // tk
    return pl.pallas_call(
        _fox_decay_kernel,
        grid=(batch,),
        in_specs=[pl.BlockSpec((seq, LANES), lambda b: (b, 0)), pl.BlockSpec((1, LANES), lambda b: (0, 0))],
        out_specs=[pl.BlockSpec((seq, LANES), lambda b: (b, 0)),
                   pl.BlockSpec((None, nkv, LANES, tk), lambda b: (b, 0, 0, 0))],
        out_shape=[jax.ShapeDtypeStruct((batch * seq, LANES), F32),
                   jax.ShapeDtypeStruct((batch, nkv, LANES, tk), F32)],
        compiler_params=_params("parallel"),
        name="fox_decay",
    )(fl, b_f_row)


def _fox_attn_kernel(q_ref, k_ref, v_ref, og_ref, ccol_ref, crow_ref, o_ref, *, scale):
    h = pl.program_id(1)
    i = pl.program_id(2)
    tq = q_ref.shape[0]
    q = q_ref[...]
    c_t = _lane_select(ccol_ref[...], h)
    row_in_tile = h % 8

    def block(j, carry, masked):
        m, l, acc = carry
        start = pl.multiple_of(j * tq, tq)
        ks = k_ref[pl.ds(start, tq), :]
        vs = v_ref[pl.ds(start, tq), :]
        c_s = crow_ref[j, pl.ds(row_in_tile, 1), :]
        s = _dot_nt(q, ks) * scale + (c_t - c_s)
        if masked:
            row = lax.broadcasted_iota(jnp.int32, s.shape, 0)
            col = lax.broadcasted_iota(jnp.int32, s.shape, 1)
            s = jnp.where(row >= col, s, -jnp.inf)
        m_new = jnp.maximum(m, jnp.max(s, axis=1, keepdims=True))
        alpha = jnp.exp(m - m_new)
        p = jnp.exp(s - m_new)
        l = alpha * l + jnp.sum(p, axis=1, keepdims=True)
        acc = alpha * acc + _dot(p.astype(BF16), vs)
        return m_new, l, acc

    init = (jnp.full((tq, 1), -1e30, F32), jnp.zeros((tq, 1), F32), jnp.zeros((tq, HEAD_DIM), F32))
    carry = lax.fori_loop(0, i, lambda j, c: block(j, c, False), init)
    _, l, acc = block(i, carry, True)
    gate = _sigmoid(og_ref[...].astype(F32))
    o_ref[...] = (acc / l * gate).astype(o_ref.dtype)


def _fox_attn(proj, ccol, crow, batch, seq, heads):
    t = proj.shape[0]
    tq = crow.shape[3]
    nq = seq // tq
    kern = functools.partial(_fox_attn_kernel, scale=HEAD_DIM ** -0.5)
    return pl.pallas_call(
        kern,
        grid=(batch, heads, nq),
        in_specs=[pl.BlockSpec((tq, HEAD_DIM), lambda b, h, i: (b * nq + i, h)),
                  pl.BlockSpec((seq, HEAD_DIM), lambda b, h, i: (b, heads + h)),
                  pl.BlockSpec((seq, HEAD_DIM), lambda b, h, i: (b, 2 * heads + h)),
                  pl.BlockSpec((tq, HEAD_DIM), lambda b, h, i: (b * nq + i, 3 * heads + h)),
                  pl.BlockSpec((tq, LANES), lambda b, h, i: (b * nq + i, 0)),
                  pl.BlockSpec((None, nq, 8, tq), lambda b, h, i: (b, 0, h // 8, 0))],
        out_specs=pl.BlockSpec((tq, HEAD_DIM), lambda b, h, i: (b * nq + i, h)),
        out_shape=jax.ShapeDtypeStruct((t, heads * HEAD_DIM), BF16),
        compiler_params=_params("parallel", "parallel", "arbitrary"),
        name="fox_attn",
    )(proj, proj, proj, proj, ccol, crow)


def _gdn_gates_kernel(ba_ref, alog_ref, dtb_ref, gcol_ref, grow_ref):
    chunk = grow_ref.shape[2]
    tri = _tri_ones(chunk)
    lane = lax.broadcasted_iota(jnp.int32, (chunk, LANES), 1)
    for n in range(grow_ref.shape[0]):
        rs = slice(n * chunk, (n + 1) * chunk)
        raw = ba_ref[rs, :]
        beta = _sigmoid(raw)
        g = -jnp.exp(alog_ref[...]) * _softplus(raw + dtb_ref[...])
        packed = jnp.where(lane < GATE_LANE, beta, _cumsum_rows(tri, g))
        gcol_ref[rs, :] = packed
        grow_ref[n] = packed.T


def _gdn_gates(ba, alog_row, dtb_row, batch, seq):
    nchunks = seq // GDN_CHUNK
    return pl.pallas_call(
        _gdn_gates_kernel,
        grid=(batch,),
        in_specs=[pl.BlockSpec((seq, LANES), lambda b: (b, 0)), pl.BlockSpec((1, LANES), lambda b: (0, 0)),
                  pl.BlockSpec((1, LANES), lambda b: (0, 0))],
        out_specs=[pl.BlockSpec((seq, LANES), lambda b: (b, 0)),
                   pl.BlockSpec((None, nchunks, LANES, GDN_CHUNK), lambda b: (b, 0, 0, 0))],
        out_shape=[jax.ShapeDtypeStruct((batch * seq, LANES), F32),
                   jax.ShapeDtypeStruct((batch, nchunks, LANES, GDN_CHUNK), F32)],
        compiler_params=_params("parallel"),
        name="gdn_gates",
    )(ba, alog_row, dtb_row)


def _unit_lower_inverse(a):
    n = a.shape[0]
    row = lax.broadcasted_iota(jnp.int32, (n, n), 0)
    col = lax.broadcasted_iota(jnp.int32, (n, n), 1)
    eye = jnp.where(row == col, 1.0, 0.0)
    p = eye - a
    m = a
    power = 2
    while power < n:
        mb = m.astype(BF16)
        m = _dot(mb, mb)
        p = p + _dot(p.astype(BF16), m.astype(BF16))
        power *= 2
    a_hi = a.astype(BF16)
    a_lo = (a - a_hi.astype(F32)).astype(BF16)
    p_hi = p.astype(BF16)
    p_lo = (p - p_hi.astype(F32)).astype(BF16)
    resid = (eye - p) - (_dot(a_hi, p_hi) + (_dot(a_hi, p_lo) + _dot(a_lo, p_hi)))
    return p + _dot(p_hi, resid.astype(BF16))


def _gdn_kernel(q_ref, k_ref, v_ref, z_ref, gcol_ref, grow_ref, onorm_ref, o_ref, state_ref, *, heads_per_k):
    kh = pl.program_id(1)
    nchunks, chunk, _ = q_ref.shape
    state_ref[...] = jnp.zeros_like(state_ref)
    row = lax.broadcasted_iota(jnp.int32, (chunk, chunk), 0)
    col = lax.broadcasted_iota(jnp.int32, (chunk, chunk), 1)

    def body(n, _):
        q = q_ref[n]
        k = k_ref[n]
        kf = k.astype(F32)
        kk = _dot_nt(k, k)
        qk = _dot_nt(q, k)
        gates = gcol_ref[n]
        for r in range(heads_per_k):
            hv = kh * heads_per_k + r
            hs = slice(r * HEAD_DIM, (r + 1) * HEAD_DIM)
            beta = _lane_select(gates, hv)
            gc = _lane_select(gates, GATE_LANE + hv)
            gc_row = grow_ref[n, pl.ds(GATE_LANE + hv, 1), :]
            g_last = gc[chunk - 1:chunk, :]
            decay = jnp.exp(jnp.where(row >= col, gc - gc_row, -1e30))
            a = jnp.where(row > col, beta * kk * decay, 0.0)
            t_inv = _unit_lower_inverse(a).astype(BF16)
            e_gc = jnp.exp(gc)
            v = v_ref[n, :, hs].astype(F32)
            rhs = jnp.concatenate([(v * beta).astype(BF16), (kf * (beta * e_gc)).astype(BF16)], axis=1)
            sol = _dot(t_inv, rhs)
            u, w = sol[:, :HEAD_DIM], sol[:, HEAD_DIM:]
            state = state_ref[r]
            sb = state.astype(BF16)
            v_new = u - _dot(w.astype(BF16), sb)
            vb = v_new.astype(BF16)
            o = _dot((q.astype(F32) * e_gc).astype(BF16), sb) + _dot((qk * decay).astype(BF16), vb)
            k_dec = (kf * jnp.exp(g_last - gc)).astype(BF16)
            state_ref[r] = state * jnp.exp(g_last) + _dot_tn(k_dec, vb)
            zf = z_ref[n, :, hs].astype(F32)
            o_ref[n, :, hs] = (_rms_rows(o, onorm_ref[...]) * _silu(zf)).astype(o_ref.dtype)
        return 0

    lax.fori_loop(0, nchunks, body, 0)


def _gdn(proj, gcol, grow, out_norm, batch, seq, k_heads, v_heads):
    t = proj.shape[0]
    nchunks = seq // GDN_CHUNK
    rep = v_heads // k_heads
    proj3 = proj.reshape(batch * nchunks, GDN_CHUNK, proj.shape[1])
    gcol3 = gcol.reshape(batch * nchunks, GDN_CHUNK, LANES)
    vw = rep * HEAD_DIM
    v_base = 2 * k_heads // rep
    z_base = v_base + v_heads // rep
    kern = functools.partial(_gdn_kernel, heads_per_k=rep)
    out = pl.pallas_call(
        kern,
        grid=(batch, k_heads),
        in_specs=[pl.BlockSpec((nchunks, GDN_CHUNK, HEAD_DIM), lambda b, h: (b, 0, h)),
                  pl.BlockSpec((nchunks, GDN_CHUNK, HEAD_DIM), lambda b, h: (b, 0, k_heads + h)),
                  pl.BlockSpec((nchunks, GDN_CHUNK, vw), lambda b, h: (b, 0, v_base + h)),
                  pl.BlockSpec((nchunks, GDN_CHUNK, vw), lambda b, h: (b, 0, z_base + h)),
                  pl.BlockSpec((nchunks, GDN_CHUNK, LANES), lambda b, h: (b, 0, 0)),
                  pl.BlockSpec((None, nchunks, LANES, GDN_CHUNK), lambda b, h: (b, 0, 0, 0)),
                  pl.BlockSpec((1, HEAD_DIM), lambda b, h: (0, 0))],
        out_specs=pl.BlockSpec((nchunks, GDN_CHUNK, vw), lambda b, h: (b, 0, h)),
        out_shape=jax.ShapeDtypeStruct((batch * nchunks, GDN_CHUNK, v_heads * HEAD_DIM), BF16),
        scratch_shapes=[pltpu.VMEM((rep, HEAD_DIM, HEAD_DIM), F32)],
        compiler_params=_params("parallel", "arbitrary"),
        name="gdn_delta_rule",
    )(proj3, proj3, proj3, proj3, gcol3, grow, out_norm)
    return out.reshape(t, v_heads * HEAD_DIM)


def _row(v, width=None):
    v = v.reshape(1, -1).astype(F32)
    if width is not None and v.shape[1] < width:
        v = jnp.pad(v, ((0, 0), (0, width - v.shape[1])))
    return v


def _mixer_gmlp(x, gain, w_in, b_in, v_norm, w_s, b_s, w_out):
    h = _gmlp_in(x, _row(gain), w_in.astype(BF16), _row(b_in))
    gated = _gmlp_gate(h, _row(v_norm), w_s, b_s.T)
    return _out_proj(gated, w_out.astype(BF16), x)


def _mixer_fox(x, gain, w_in, b_f, q_norm, k_norm, w_out, batch, seq):
    heads = b_f.shape[0]
    dim = heads * HEAD_DIM
    w_main = w_in[:, :4 * dim].astype(BF16)
    w_f = jnp.pad(w_in[:, 4 * dim:], ((0, 0), (0, LANES - heads))).astype(BF16)
    head_gain = jnp.concatenate([jnp.tile(q_norm, heads), jnp.tile(k_norm, heads), jnp.ones((2 * dim,), F32)])
    proj, fl = _fox_in(x, _row(gain), w_main, _row(head_gain), w_f, 2 * dim)
    tq = min(ATTN_TILE, seq)
    ccol, crow = _fox_decay(fl, _row(b_f, LANES), batch, seq, tq)
    o = _fox_attn(proj, ccol, crow, batch, seq, heads)
    return _out_proj(o, w_out.astype(BF16), x)


def _mixer_gdn(x, gain, w_in, conv_w, a_log, dt_bias, out_norm, w_out, batch, seq):
    v_heads = a_log.shape[0]
    dv = v_heads * HEAD_DIM
    n_qkv = conv_w.shape[1]
    dk = (n_qkv - dv) // 2
    k_heads = dk // HEAD_DIM
    assert v_heads <= GATE_LANE
    w_main = w_in[:, :n_qkv + dv].astype(BF16)
    w_b = w_in[:, n_qkv + dv:n_qkv + dv + v_heads]
    w_a = w_in[:, n_qkv + dv + v_heads:]
    pad = GATE_LANE - v_heads
    w_ba = jnp.pad(jnp.concatenate([jnp.pad(w_b, ((0, 0), (0, pad))), w_a], axis=1),
                   ((0, 0), (0, LANES - GATE_LANE - v_heads))).astype(BF16)
    conv_full = jnp.pad(conv_w, ((0, 0), (0, dv)))
    scale_row = jnp.concatenate([jnp.full((dk,), HEAD_DIM ** -0.5, F32), jnp.ones((dk + 2 * dv,), F32)])
    proj, ba = _gdn_in(x, _row(gain), w_main, conv_full, _row(scale_row), w_ba, 2 * dk, n_qkv, seq)
    alog_row = jnp.pad(_row(a_log), ((0, 0), (GATE_LANE, LANES - GATE_LANE - v_heads)))
    dtb_row = jnp.pad(_row(dt_bias), ((0, 0), (GATE_LANE, LANES - GATE_LANE - v_heads)))
    gcol, grow = _gdn_gates(ba, alog_row, dtb_row, batch, seq)
    o = _gdn(proj, gcol, grow, _row(out_norm), batch, seq, k_heads, v_heads)
    return _out_proj(o, w_out.astype(BF16), x)


def _conv_ffn(x, gain, w_gate, w_up, conv_w, conv_b, w_down, seq):
    h = _ffn_in(x, _row(gain), w_gate.astype(BF16), w_up.astype(BF16), conv_w, _row(conv_b), seq)
    return _out_proj(h, w_down.astype(BF16), x)


def kernel(x, norm_mix, norm_ffn, ffn_w_gate, ffn_w_up, ffn_conv_w, ffn_conv_b, ffn_w_down, a_w_in, a_b_in, a_v_norm, a_w_s, a_b_s, a_w_out, b_w_in, b_b_f, b_q_norm, b_k_norm, b_w_out, c_w_in, c_conv_w, c_a_log, c_dt_bias, c_out_norm, c_w_out):
    batch, seq, d = x.shape
    depth = norm_mix.shape[0]
    xt = x.reshape(batch * seq, d)
    for i in range(depth):
        kind, j = i % 3, i // 3
        if kind == 0:
            xt = _mixer_gmlp(xt, norm_mix[i], a_w_in[j], a_b_in[j], a_v_norm[j], a_w_s[j], a_b_s[j], a_w_out[j])
        elif kind == 1:
            xt = _mixer_fox(xt, norm_mix[i], b_w_in[j], b_b_f[j], b_q_norm[j], b_k_norm[j], b_w_out[j], batch, seq)
        else:
            xt = _mixer_gdn(xt, norm_mix[i], c_w_in[j], c_conv_w[j], c_a_log[j], c_dt_bias[j], c_out_norm[j],
                            c_w_out[j], batch, seq)
        xt = _conv_ffn(xt, norm_ffn[i], ffn_w_gate[i], ffn_w_up[i], ffn_conv_w[i], ffn_conv_b[i], ffn_w_down[i], seq)
    return xt.reshape(batch, seq, d)
```

```python
import functools

import jax
import jax.numpy as jnp
from jax import lax
from jax.experimental import pallas as pl
from jax.experimental.pallas import tpu as pltpu

RMS_EPS = 1e-6
HEAD_DIM = 128
GMLP_CHUNK = 128
GDN_CHUNK = 128
LANES = 128
HALO = 16
GATE_LANE = 32

TILE_M = 1024
TILE_N = 512
ATTN_TILE = 512
GMLP_TILE_M = 256
GDN_K_HEADS_PER_STEP = 2
GDN_CHUNKS_PER_ITER = 4
VMEM_LIMIT_BYTES = 56 * 1024 * 1024

F32 = jnp.float32
BF16 = jnp.bfloat16


def _params(*semantics):
    return pltpu.CompilerParams(dimension_semantics=semantics, vmem_limit_bytes=VMEM_LIMIT_BYTES)


def _dot(a, b):
    return jnp.dot(a, b, preferred_element_type=F32)


def _dot_nt(a, b):
    return lax.dot_general(a, b, (((1,), (1,)), ((), ())), preferred_element_type=F32)


def _dot_tn(a, b):
    return lax.dot_general(a, b, (((0,), (0,)), ((), ())), preferred_element_type=F32)


def _rms_rows(xf, gain):
    ms = jnp.mean(xf * xf, axis=-1, keepdims=True)
    return xf * lax.rsqrt(ms + RMS_EPS) * gain


def _sigmoid(x):
    return 1.0 / (1.0 + jnp.exp(-x))


def _silu(x):
    return x * _sigmoid(x)


def _softplus(x):
    return jnp.maximum(x, 0.0) + jnp.log(1.0 + jnp.exp(-jnp.abs(x)))


def _gelu_tanh(x):
    return 0.5 * x * (1.0 + jnp.tanh(0.7978845608028654 * (x + 0.044715 * (x * x * x))))


def _split3(x):
    hi = x.astype(BF16)
    r1 = x - hi.astype(F32)
    mid = r1.astype(BF16)
    lo = (r1 - mid.astype(F32)).astype(BF16)
    return hi, mid, lo


def _lane_select(block, idx):
    lane = lax.broadcasted_iota(jnp.int32, block.shape, 1)
    return jnp.sum(jnp.where(lane == idx, block, 0.0), axis=1, keepdims=True)


def _causal_conv_rows(acc, prev, w_ref):
    taps = w_ref.shape[0]
    out = acc * w_ref[taps - 1:taps, :]
    top = jnp.concatenate([prev[HALO - 8:, :], acc[:8, :]], axis=0)
    for s in range(1, taps):
        rolled = pltpu.roll(acc, s, 0)
        rolled_top = pltpu.roll(top, s, 0)[8:16, :]
        shifted = jnp.concatenate([rolled_top, rolled[8:, :]], axis=0)
        out = out + shifted * w_ref[taps - 1 - s:taps - s, :]
    return out


def _store_normed(x_ref, g_ref, xn_ref):
    xn_ref[...] = _rms_rows(x_ref[...], g_ref[...]).astype(BF16)


def _gmlp_in_kernel(x_ref, g_ref, w_ref, b_ref, o_ref, xn_ref):
    @pl.when(pl.program_id(1) == 0)
    def _():
        _store_normed(x_ref, g_ref, xn_ref)

    acc = _dot(xn_ref[...], w_ref[...]) + b_ref[...]
    o_ref[...] = _gelu_tanh(acc).astype(o_ref.dtype)


def _fox_in_kernel(x_ref, g_ref, w_ref, gain_ref, w2_ref, o_ref, o2_ref, xn_ref, *, n_norm_tiles):
    j = pl.program_id(1)

    @pl.when(j == 0)
    def _():
        _store_normed(x_ref, g_ref, xn_ref)
        o2_ref[...] = _dot(xn_ref[...], w2_ref[...])

    @pl.when(j < n_norm_tiles)
    def _():
        acc = _dot(xn_ref[...], w_ref[...])
        for c in range(acc.shape[1] // HEAD_DIM):
            sl = slice(c * HEAD_DIM, (c + 1) * HEAD_DIM)
            o_ref[:, sl] = _rms_rows(acc[:, sl], gain_ref[:, sl]).astype(o_ref.dtype)

    @pl.when(j >= n_norm_tiles)
    def _():
        o_ref[...] = _dot(xn_ref[...], w_ref[...]).astype(o_ref.dtype)


def _seq_start_mask(prev, tiles_per_seq):
    first = (pl.program_id(0) % tiles_per_seq) == 0
    return jnp.where(first, 0.0, prev)


def _gdn_in_kernel(x_ref, xp_ref, g_ref, w_ref, cw_ref, scale_ref, w2_ref, o_ref, o2_ref, xn_ref, xnp_ref,
                   *, n_qk_tiles, n_conv_tiles, tiles_per_seq):
    j = pl.program_id(1)

    @pl.when(j == 0)
    def _():
        _store_normed(x_ref, g_ref, xn_ref)
        _store_normed(xp_ref, g_ref, xnp_ref)
        o2_ref[...] = _dot(xn_ref[...], w2_ref[...])

    def conv_silu():
        acc = _dot(xn_ref[...], w_ref[...])
        prev = _seq_start_mask(_dot(xnp_ref[...], w_ref[...]), tiles_per_seq)
        return _silu(_causal_conv_rows(acc, prev, cw_ref))

    @pl.when(j < n_qk_tiles)
    def _():
        y = conv_silu()
        for c in range(y.shape[1] // HEAD_DIM):
            sl = slice(c * HEAD_DIM, (c + 1) * HEAD_DIM)
            blk = y[:, sl]
            inv = lax.rsqrt(jnp.sum(blk * blk, axis=-1, keepdims=True) + RMS_EPS)
            o_ref[:, sl] = (blk * inv * scale_ref[:, sl]).astype(o_ref.dtype)

    @pl.when(jnp.logical_and(j >= n_qk_tiles, j < n_conv_tiles))
    def _():
        o_ref[...] = conv_silu().astype(o_ref.dtype)

    @pl.when(j >= n_conv_tiles)
    def _():
        o_ref[...] = _dot(xn_ref[...], w_ref[...]).astype(o_ref.dtype)


def _ffn_in_kernel(x_ref, xp_ref, g_ref, wg_ref, wu_ref, cw_ref, cb_ref, o_ref, xn_ref, xnp_ref, *, tiles_per_seq):
    @pl.when(pl.program_id(1) == 0)
    def _():
        _store_normed(x_ref, g_ref, xn_ref)
        _store_normed(xp_ref, g_ref, xnp_ref)

    xn = xn_ref[...]
    gate = _dot(xn, wg_ref[...])
    up = _dot(xn, wu_ref[...])
    prev = _seq_start_mask(_dot(xnp_ref[...], wg_ref[...]), tiles_per_seq)
    gate = _causal_conv_rows(gate, prev, cw_ref) + cb_ref[...]
    o_ref[...] = (_silu(gate) * up).astype(o_ref.dtype)


def _row_tile(t):
    return min(TILE_M, t)


def _x_specs(tm, d, halo):
    specs = [pl.BlockSpec((tm, d), lambda i, j: (i, 0))]
    if halo:
        per = tm // HALO
        specs.append(pl.BlockSpec((HALO, d), lambda i, j: (jnp.maximum(i * per - 1, 0), 0)))
    return specs


def _col(tn):
    return pl.BlockSpec((1, tn), lambda i, j: (0, j))


def _gmlp_in(x, gain, w, b):
    t, d = x.shape
    n = w.shape[1]
    tm, tn = _row_tile(t), min(TILE_N, n)
    return pl.pallas_call(
        _gmlp_in_kernel,
        grid=(t // tm, n // tn),
        in_specs=_x_specs(tm, d, False) + [pl.BlockSpec((1, d), lambda i, j: (0, 0)),
                                           pl.BlockSpec((d, tn), lambda i, j: (0, j)), _col(tn)],
        out_specs=pl.BlockSpec((tm, tn), lambda i, j: (i, j)),
        out_shape=jax.ShapeDtypeStruct((t, n), BF16),
        scratch_shapes=[pltpu.VMEM((tm, d), BF16)],
        compiler_params=_params("parallel", "arbitrary"),
        name="gmlp_in",
    )(x, gain, w, b)


def _fox_in(x, gain, w, head_gain, w2, n_norm_cols):
    t, d = x.shape
    n = w.shape[1]
    tm, tn = _row_tile(t), min(TILE_N, n_norm_cols)
    kern = functools.partial(_fox_in_kernel, n_norm_tiles=n_norm_cols // tn)
    return pl.pallas_call(
        kern,
        grid=(t // tm, n // tn),
        in_specs=_x_specs(tm, d, False) + [pl.BlockSpec((1, d), lambda i, j: (0, 0)),
                                           pl.BlockSpec((d, tn), lambda i, j: (0, j)), _col(tn),
                                           pl.BlockSpec((d, LANES), lambda i, j: (0, 0))],
        out_specs=[pl.BlockSpec((tm, tn), lambda i, j: (i, j)), pl.BlockSpec((tm, LANES), lambda i, j: (i, 0))],
        out_shape=[jax.ShapeDtypeStruct((t, n), BF16), jax.ShapeDtypeStruct((t, LANES), F32)],
        scratch_shapes=[pltpu.VMEM((tm, d), BF16)],
        compiler_params=_params("parallel", "arbitrary"),
        name="fox_in",
    )(x, gain, w, head_gain, w2)


def _gdn_in(x, gain, w, conv_w, scale_row, w2, n_qk_cols, n_conv_cols, seq):
    t, d = x.shape
    n = w.shape[1]
    tm, tn = _row_tile(min(t, seq)), min(TILE_N, n_qk_cols)
    kern = functools.partial(_gdn_in_kernel, n_qk_tiles=n_qk_cols // tn, n_conv_tiles=n_conv_cols // tn,
                             tiles_per_seq=seq // tm)
    return pl.pallas_call(
        kern,
        grid=(t // tm, n // tn),
        in_specs=_x_specs(tm, d, True) + [pl.BlockSpec((1, d), lambda i, j: (0, 0)),
                                          pl.BlockSpec((d, tn), lambda i, j: (0, j)),
                                          pl.BlockSpec((conv_w.shape[0], tn), lambda i, j: (0, j)), _col(tn),
                                          pl.BlockSpec((d, LANES), lambda i, j: (0, 0))],
        out_specs=[pl.BlockSpec((tm, tn), lambda i, j: (i, j)), pl.BlockSpec((tm, LANES), lambda i, j: (i, 0))],
        out_shape=[jax.ShapeDtypeStruct((t, n), BF16), jax.ShapeDtypeStruct((t, LANES), F32)],
        scratch_shapes=[pltpu.VMEM((tm, d), BF16), pltpu.VMEM((HALO, d), BF16)],
        compiler_params=_params("parallel", "arbitrary"),
        name="gdn_in",
    )(x, x, gain, w, conv_w, scale_row, w2)


def _ffn_in(x, gain, wg, wu, conv_w, conv_b, seq):
    t, d = x.shape
    n = wg.shape[1]
    tm, tn = _row_tile(min(t, seq)), min(TILE_N, n)
    while n % tn:
        tn -= LANES
    kern = functools.partial(_ffn_in_kernel, tiles_per_seq=seq // tm)
    return pl.pallas_call(
        kern,
        grid=(t // tm, n // tn),
        in_specs=_x_specs(tm, d, True) + [pl.BlockSpec((1, d), lambda i, j: (0, 0)),
                                          pl.BlockSpec((d, tn), lambda i, j: (0, j)),
                                          pl.BlockSpec((d, tn), lambda i, j: (0, j)),
                                          pl.BlockSpec((conv_w.shape[0], tn), lambda i, j: (0, j)), _col(tn)],
        out_specs=pl.BlockSpec((tm, tn), lambda i, j: (i, j)),
        out_shape=jax.ShapeDtypeStruct((t, n), BF16),
        scratch_shapes=[pltpu.VMEM((tm, d), BF16), pltpu.VMEM((HALO, d), BF16)],
        compiler_params=_params("parallel", "arbitrary"),
        name="ffn_in",
    )(x, x, gain, wg, wu, conv_w, conv_b)


def _out_proj_kernel(a_ref, w_ref, x_ref, o_ref):
    o_ref[...] = x_ref[...] + _dot(a_ref[...], w_ref[...])


def _out_proj(a, w, x):
    t, k = a.shape
    d = w.shape[1]
    tm, tn = _row_tile(t), min(TILE_N, d)
    return pl.pallas_call(
        _out_proj_kernel,
        grid=(t // tm, d // tn),
        in_specs=[pl.BlockSpec((tm, k), lambda i, j: (i, 0)), pl.BlockSpec((k, tn), lambda i, j: (0, j)),
                  pl.BlockSpec((tm, tn), lambda i, j: (i, j))],
        out_specs=pl.BlockSpec((tm, tn), lambda i, j: (i, j)),
        out_shape=jax.ShapeDtypeStruct((t, d), F32),
        compiler_params=_params("parallel", "arbitrary"),
        name="out_proj",
    )(a, w, x)


def _gmlp_gate_kernel(u_ref, v_ref, vg_ref, ws_ref, bst_ref, o_ref, vn_ref):
    vn_ref[...] = _rms_rows(v_ref[...].astype(F32), vg_ref[...]).astype(BF16)
    tm = u_ref.shape[0]
    groups = ws_ref.shape[0]
    row = lax.broadcasted_iota(jnp.int32, (GMLP_CHUNK, GMLP_CHUNK), 0)
    col = lax.broadcasted_iota(jnp.int32, (GMLP_CHUNK, GMLP_CHUNK), 1)
    for g in range(groups):
        cs = slice(g * HEAD_DIM, (g + 1) * HEAD_DIM)
        w_causal = jnp.where(row >= col, ws_ref[g], 0.0).astype(BF16)
        bias = bst_ref[:, g:g + 1]
        for c in range(tm // GMLP_CHUNK):
            rs = slice(c * GMLP_CHUNK, (c + 1) * GMLP_CHUNK)
            sv = _dot(w_causal, vn_ref[rs, cs]) + bias
            o_ref[rs, cs] = (u_ref[rs, cs].astype(F32) * sv).astype(o_ref.dtype)


def _gmlp_gate(h, v_gain, w_s, b_s_t):
    t, two_a = h.shape
    a = two_a // 2
    tm = min(GMLP_TILE_M, t)
    groups = w_s.shape[0]
    return pl.pallas_call(
        _gmlp_gate_kernel,
        grid=(t // tm,),
        in_specs=[pl.BlockSpec((tm, a), lambda i: (i, 0)), pl.BlockSpec((tm, a), lambda i: (i, 1)),
                  pl.BlockSpec((1, a), lambda i: (0, 0)),
                  pl.BlockSpec((groups, GMLP_CHUNK, GMLP_CHUNK), lambda i: (0, 0, 0)),
                  pl.BlockSpec((GMLP_CHUNK, groups), lambda i: (0, 0))],
        out_specs=pl.BlockSpec((tm, a), lambda i: (i, 0)),
        out_shape=jax.ShapeDtypeStruct((t, a), BF16),
        scratch_shapes=[pltpu.VMEM((tm, a), BF16)],
        compiler_params=_params("parallel"),
        name="gmlp_gate",
    )(h, h, v_gain, w_s, b_s_t)


def _tri_ones(n):
    row = lax.broadcasted_iota(jnp.int32, (n, n), 0)
    col = lax.broadcasted_iota(jnp.int32, (n, n), 1)
    return jnp.where(row >= col, 1.0, 0.0).astype(BF16)


def _cumsum_rows(tri, x):
    hi, mid, lo = _split3(x)
    return _dot(tri, hi) + _dot(tri, mid) + _dot(tri, lo)


def _fox_decay_kernel(fl_ref, bf_ref, ccol_ref, crow_ref):
    tk = crow_ref.shape[2]
    tri = _tri_ones(tk)
    carry = jnp.zeros((1, LANES), F32)
    for n in range(crow_ref.shape[0]):
        rs = slice(n * tk, (n + 1) * tk)
        log_f = -_softplus(-(fl_ref[rs, :] + bf_ref[...]))
        c = _cumsum_rows(tri, log_f) + carry
        carry = c[tk - 1:tk, :]
        ccol_ref[rs, :] = c
        crow_ref[n] = c.T


def _fox_decay(fl, b_f_row, batch, seq, tk):
    nkv = seq // tk
    return pl.pallas_call(
        _fox_decay_kernel,
        grid=(batch,),
        in_specs=[pl.BlockSpec((seq, LANES), lambda b: (b, 0)), pl.BlockSpec((1, LANES), lambda b: (0, 0))],
        out_specs=[pl.BlockSpec((seq, LANES), lambda b: (b, 0)),
                   pl.BlockSpec((None, nkv, LANES, tk), lambda b: (b, 0, 0, 0))],
        out_shape=[jax.ShapeDtypeStruct((batch * seq, LANES), F32),
                   jax.ShapeDtypeStruct((batch, nkv, LANES, tk), F32)],
        compiler_params=_params("parallel"),
        name="fox_decay",
    )(fl, b_f_row)


def _attn_chains(nq, n_chains=2):
    chains = [[] for _ in range(n_chains)]
    for i in reversed(range(nq)):
        min(chains, key=len).extend((i, j) for j in range(i + 1))
    return chains


def _fox_attn_kernel(q_ref, k_ref, v_ref, og_ref, ccol_ref, crow_ref, o_ref, *, tile, scale):
    h = pl.program_id(1)
    nq = q_ref.shape[0] // tile
    log2e = 1.4426950408889634
    row_in_tile = h % 8
    row = lax.broadcasted_iota(jnp.int32, (tile, tile), 0)
    col = lax.broadcasted_iota(jnp.int32, (tile, tile), 1)
    chains = _attn_chains(nq)
    state = [None] * len(chains)

    def rows(i):
        return pl.ds(i * tile, tile)

    for step in range(max(len(c) for c in chains)):
        live = [(n, c[step]) for n, c in enumerate(chains) if step < len(c)]
        scores = [_dot_nt(q_ref[rows(i), :], k_ref[rows(j), :]) for _, (i, j) in live]
        probs = []
        for (n, (i, j)), s in zip(live, scores):
            c_t = _lane_select(ccol_ref[rows(i), :], h) * log2e
            c_s = crow_ref[j, pl.ds(row_in_tile, 1), :] * log2e
            t = s * (scale * log2e) - c_s
            if i == j:
                t = jnp.where(row >= col, t, -1e30)
            m_new = jnp.max(t, axis=1, keepdims=True) + c_t
            if j > 0:
                m_old, l_old, acc_old = state[n]
                m_new = jnp.maximum(m_old, m_new)
            p = jnp.exp2(t - (m_new - c_t))
            l_new = jnp.sum(p, axis=1, keepdims=True)
            if j > 0:
                alpha = jnp.exp2(m_old - m_new)
                l_new = alpha * l_old + l_new
                state[n] = (m_new, l_new, alpha * acc_old)
            else:
                state[n] = (m_new, l_new, None)
            probs.append(p.astype(BF16))
        updates = [_dot(p, v_ref[rows(j), :]) for (_, (_, j)), p in zip(live, probs)]
        for (n, (i, j)), pv in zip(live, updates):
            m, l, acc = state[n]
            acc = pv if acc is None else acc + pv
            state[n] = (m, l, acc)
            if i == j:
                gate = _sigmoid(og_ref[rows(i), :].astype(F32))
                o_ref[rows(i), :] = (acc / l * gate).astype(o_ref.dtype)


def _fox_attn(proj, ccol, crow, batch, seq, heads):
    t = proj.shape[0]
    tile = crow.shape[3]
    nq = seq // tile
    kern = functools.partial(_fox_attn_kernel, tile=tile, scale=HEAD_DIM ** -0.5)

    def head_block(offset):
        return pl.BlockSpec((seq, HEAD_DIM), lambda b, h: (b, offset + h))

    return pl.pallas_call(
        kern,
        grid=(batch, heads),
        in_specs=[head_block(0), head_block(heads), head_block(2 * heads), head_block(3 * heads),
                  pl.BlockSpec((seq, LANES), lambda b, h: (b, 0)),
                  pl.BlockSpec((None, nq, 8, tile), lambda b, h: (b, 0, h // 8, 0))],
        out_specs=head_block(0),
        out_shape=jax.ShapeDtypeStruct((t, heads * HEAD_DIM), BF16),
        compiler_params=_params("parallel", "arbitrary"),
        name="fox_attn",
    )(proj, proj, proj, proj, ccol, crow)


def _gdn_gates_kernel(ba_ref, alog_ref, dtb_ref, gcol_ref, grow_ref):
    chunk = grow_ref.shape[2]
    tri = _tri_ones(chunk)
    lane = lax.broadcasted_iota(jnp.int32, (chunk, LANES), 1)
    for n in range(grow_ref.shape[0]):
        rs = slice(n * chunk, (n + 1) * chunk)
        raw = ba_ref[rs, :]
        beta = _sigmoid(raw)
        g = -jnp.exp(alog_ref[...]) * _softplus(raw + dtb_ref[...])
        packed = jnp.where(lane < GATE_LANE, beta, _cumsum_rows(tri, g))
        gcol_ref[rs, :] = packed
        grow_ref[n] = packed.T


def _gdn_gates(ba, alog_row, dtb_row, batch, seq):
    nchunks = seq // GDN_CHUNK
    return pl.pallas_call(
        _gdn_gates_kernel,
        grid=(batch,),
        in_specs=[pl.BlockSpec((seq, LANES), lambda b: (b, 0)), pl.BlockSpec((1, LANES), lambda b: (0, 0)),
                  pl.BlockSpec((1, LANES), lambda b: (0, 0))],
        out_specs=[pl.BlockSpec((seq, LANES), lambda b: (b, 0)),
                   pl.BlockSpec((None, nchunks, LANES, GDN_CHUNK), lambda b: (b, 0, 0, 0))],
        out_shape=[jax.ShapeDtypeStruct((batch * seq, LANES), F32),
                   jax.ShapeDtypeStruct((batch, nchunks, LANES, GDN_CHUNK), F32)],
        compiler_params=_params("parallel"),
        name="gdn_gates",
    )(ba, alog_row, dtb_row)


def _block_diag(pack):
    n = pack.shape[0]
    reps = pack.shape[1] // n
    zero = jnp.zeros((n, n), pack.dtype)
    rows = []
    for r in range(reps):
        blk = pack[:, r * n:(r + 1) * n]
        rows.append(jnp.concatenate([blk if c == r else zero for c in range(reps)], axis=1))
    return jnp.concatenate(rows, axis=0)


def _unit_lower_inverses(mats):
    n = mats[0].shape[0]
    row = lax.broadcasted_iota(jnp.int32, mats[0].shape, 0)
    col = lax.broadcasted_iota(jnp.int32, mats[0].shape, 1)
    eye = jnp.where(row == col % n, 1.0, 0.0)
    bs = [-a for a in mats]
    b_his = [b.astype(BF16) for b in bs]
    ps = [eye + b for b in bs]
    ms = [_dot(b_hi, _block_diag(b_hi)) for b_hi in b_his]
    power = 2
    while power < n:
        mbs = [m.astype(BF16) for m in ms]
        if 2 * power < n:
            both = [_dot(jnp.concatenate([p.astype(BF16), mb], axis=0), _block_diag(mb)) for p, mb in zip(ps, mbs)]
            ps = [p + x[:n] for p, x in zip(ps, both)]
            ms = [x[n:] for x in both]
        else:
            ps = [p + _dot(p.astype(BF16), _block_diag(mb)) for p, mb in zip(ps, mbs)]
        power *= 2
    b_los = [(b - b_hi.astype(F32)).astype(BF16) for b, b_hi in zip(bs, b_his)]
    p_his = [p.astype(BF16) for p in ps]
    p_los = [(p - p_hi.astype(F32)).astype(BF16) for p, p_hi in zip(ps, p_his)]
    bps = [_dot(jnp.concatenate([b_hi, b_lo], axis=0), _block_diag(p_hi))
           for b_hi, b_lo, p_hi in zip(b_his, b_los, p_his)]
    cross = [_dot(b_hi, _block_diag(p_lo)) for b_hi, p_lo in zip(b_his, p_los)]
    resids = [(eye - p) + (bp[:n] + (bp[n:] + x)) for p, bp, x in zip(ps, bps, cross)]
    return [p + _dot(p_hi, _block_diag(r.astype(BF16))) for p, p_hi, r in zip(ps, p_his, resids)]


def _gdn_kernel(q_ref, k_ref, v_ref, z_ref, gcol_ref, grow_ref, onorm_ref, o_ref,
                u_ref, w_ref, attn_ref, state_ref, *, k_per_step, rep, chunks_per_iter):
    step = pl.program_id(1)
    nchunks, chunk, _ = q_ref.shape
    row = lax.broadcasted_iota(jnp.int32, (chunk, chunk), 0)
    col = lax.broadcasted_iota(jnp.int32, (chunk, chunk), 1)

    def head_slice(idx):
        return slice(idx * HEAD_DIM, (idx + 1) * HEAD_DIM)

    def pass1(i, _):
        jobs = [(i * chunks_per_iter + c, j) for c in range(chunks_per_iter) for j in range(k_per_step)]
        qs = [q_ref[n, :, head_slice(j)] for n, j in jobs]
        ks = [k_ref[n, :, head_slice(j)] for n, j in jobs]
        kks = [_dot_nt(k, k) for k in ks]
        qks = [_dot_nt(q, k) for q, k in zip(qs, ks)]
        a_packs, rhs = [], []
        for (n, j), k, kk, qk in zip(jobs, ks, kks, qks):
            gates = gcol_ref[n]
            kf = k.astype(F32)
            a_blocks, attn_blocks, job_rhs = [], [], []
            for r in range(rep):
                hv = (step * k_per_step + j) * rep + r
                beta = _lane_select(gates, hv)
                gc = _lane_select(gates, GATE_LANE + hv)
                gc_row = grow_ref[n, pl.ds(GATE_LANE + hv, 1), :]
                decay = jnp.exp(jnp.where(row >= col, gc - gc_row, -1e30))
                a_blocks.append(jnp.where(row > col, beta * kk * decay, 0.0))
                attn_blocks.append((qk * decay).astype(BF16))
                v = v_ref[n, :, head_slice(j * rep + r)].astype(F32)
                job_rhs.append(jnp.concatenate([(v * beta).astype(BF16),
                                                (kf * (beta * jnp.exp(gc))).astype(BF16)], axis=1))
            attn_ref[n, j] = jnp.concatenate(attn_blocks, axis=1)
            a_packs.append(jnp.concatenate(a_blocks, axis=1))
            rhs.append(job_rhs)
        t_invs = [t.astype(BF16) for t in _unit_lower_inverses(a_packs)]
        sols = [[_dot(t_inv[:, head_slice(r)], job_rhs[r]) for r in range(rep)]
                for t_inv, job_rhs in zip(t_invs, rhs)]
        for (n, j), job_sols in zip(jobs, sols):
            u_ref[n, j] = jnp.concatenate([s[:, :HEAD_DIM] for s in job_sols], axis=1)
            w_ref[n, j] = jnp.concatenate([s[:, HEAD_DIM:] for s in job_sols], axis=1).astype(BF16)
        return 0

    def pass2(n, _):
        gates = gcol_ref[n]
        heads = range(k_per_step)
        qfs = [q_ref[n, :, head_slice(j)].astype(F32) for j in heads]
        kfs = [k_ref[n, :, head_slice(j)].astype(F32) for j in heads]
        gcs = [[_lane_select(gates, GATE_LANE + (step * k_per_step + j) * rep + r) for r in range(rep)] for j in heads]
        states = [[state_ref[j, r] for r in range(rep)] for j in heads]
        q_decs = [jnp.concatenate([(qfs[j] * jnp.exp(gc)).astype(BF16) for gc in gcs[j]], axis=1) for j in heads]
        s_bds = [_block_diag(jnp.concatenate([s.astype(BF16) for s in states[j]], axis=1)) for j in heads]
        boths = [_dot(jnp.concatenate([w_ref[n, j], q_decs[j]], axis=0), s_bds[j]) for j in heads]
        vbs = [(u_ref[n, j] - boths[j][:chunk]).astype(BF16) for j in heads]
        g_lasts = [[gc[chunk - 1:chunk, :] for gc in gcs[j]] for j in heads]
        k_decs = [[(kfs[j] * jnp.exp(g_lasts[j][r] - gcs[j][r])).astype(BF16) for r in range(rep)] for j in heads]
        updates = [[_dot_tn(k_decs[j][r], vbs[j][:, head_slice(r)]) for r in range(rep)] for j in heads]
        for j in heads:
            for r in range(rep):
                state_ref[j, r] = states[j][r] * jnp.exp(g_lasts[j][r]) + updates[j][r]
        outs = [boths[j][chunk:] + _dot(attn_ref[n, j], _block_diag(vbs[j])) for j in heads]
        for j in heads:
            for r in range(rep):
                hs = head_slice(j * rep + r)
                zf = z_ref[n, :, hs].astype(F32)
                o_ref[n, :, hs] = (_rms_rows(outs[j][:, head_slice(r)], onorm_ref[...]) * _silu(zf)).astype(o_ref.dtype)
        return 0

    lax.fori_loop(0, nchunks // chunks_per_iter, pass1, 0)
    state_ref[...] = jnp.zeros_like(state_ref)
    lax.fori_loop(0, nchunks, pass2, 0)


def _gdn(proj, gcol, grow, out_norm, batch, seq, k_heads, v_heads):
    t = proj.shape[0]
    nchunks = seq // GDN_CHUNK
    rep = v_heads // k_heads
    kps = min(GDN_K_HEADS_PER_STEP, k_heads)
    steps = k_heads // kps
    proj3 = proj.reshape(batch * nchunks, GDN_CHUNK, proj.shape[1])
    gcol3 = gcol.reshape(batch * nchunks, GDN_CHUNK, LANES)
    kw, vw = kps * HEAD_DIM, kps * rep * HEAD_DIM
    v_base = 2 * k_heads * HEAD_DIM // vw
    z_base = v_base + v_heads * HEAD_DIM // vw
    kern = functools.partial(_gdn_kernel, k_per_step=kps, rep=rep,
                             chunks_per_iter=min(GDN_CHUNKS_PER_ITER, nchunks))
    pack = rep * HEAD_DIM
    out = pl.pallas_call(
        kern,
        grid=(batch, steps),
        in_specs=[pl.BlockSpec((nchunks, GDN_CHUNK, kw), lambda b, h: (b, 0, h)),
                  pl.BlockSpec((nchunks, GDN_CHUNK, kw), lambda b, h: (b, 0, steps + h)),
                  pl.BlockSpec((nchunks, GDN_CHUNK, vw), lambda b, h: (b, 0, v_base + h)),
                  pl.BlockSpec((nchunks, GDN_CHUNK, vw), lambda b, h: (b, 0, z_base + h)),
                  pl.BlockSpec((nchunks, GDN_CHUNK, LANES), lambda b, h: (b, 0, 0)),
                  pl.BlockSpec((None, nchunks, LANES, GDN_CHUNK), lambda b, h: (b, 0, 0, 0)),
                  pl.BlockSpec((1, HEAD_DIM), lambda b, h: (0, 0))],
        out_specs=pl.BlockSpec((nchunks, GDN_CHUNK, vw), lambda b, h: (b, 0, h)),
        out_shape=jax.ShapeDtypeStruct((batch * nchunks, GDN_CHUNK, v_heads * HEAD_DIM), BF16),
        scratch_shapes=[pltpu.VMEM((nchunks, kps, GDN_CHUNK, pack), F32),
                        pltpu.VMEM((nchunks, kps, GDN_CHUNK, pack), BF16),
                        pltpu.VMEM((nchunks, kps, GDN_CHUNK, pack), BF16),
                        pltpu.VMEM((kps, rep, HEAD_DIM, HEAD_DIM), F32)],
        compiler_params=_params("parallel", "arbitrary"),
        name="gdn_delta_rule",
    )(proj3, proj3, proj3, proj3, gcol3, grow, out_norm)
    return out.reshape(t, v_heads * HEAD_DIM)


def _row(v, width=None):
    v = v.reshape(1, -1).astype(F32)
    if width is not None and v.shape[1] < width:
        v = jnp.pad(v, ((0, 0), (0, width - v.shape[1])))
    return v


def _mixer_gmlp(x, gain, w_in, b_in, v_norm, w_s, b_s, w_out):
    h = _gmlp_in(x, _row(gain), w_in.astype(BF16), _row(b_in))
    gated = _gmlp_gate(h, _row(v_norm), w_s, b_s.T)
    return _out_proj(gated, w_out.astype(BF16), x)


def _mixer_fox(x, gain, w_in, b_f, q_norm, k_norm, w_out, batch, seq):
    heads = b_f.shape[0]
    dim = heads * HEAD_DIM
    w_main = w_in[:, :4 * dim].astype(BF16)
    w_f = jnp.pad(w_in[:, 4 * dim:], ((0, 0), (0, LANES - heads))).astype(BF16)
    head_gain = jnp.concatenate([jnp.tile(q_norm, heads), jnp.tile(k_norm, heads), jnp.ones((2 * dim,), F32)])
    proj, fl = _fox_in(x, _row(gain), w_main, _row(head_gain), w_f, 2 * dim)
    tq = min(ATTN_TILE, seq)
    ccol, crow = _fox_decay(fl, _row(b_f, LANES), batch, seq, tq)
    o = _fox_attn(proj, ccol, crow, batch, seq, heads)
    return _out_proj(o, w_out.astype(BF16), x)


def _mixer_gdn(x, gain, w_in, conv_w, a_log, dt_bias, out_norm, w_out, batch, seq):
    v_heads = a_log.shape[0]
    dv = v_heads * HEAD_DIM
    n_qkv = conv_w.shape[1]
    dk = (n_qkv - dv) // 2
    k_heads = dk // HEAD_DIM
    assert v_heads <= GATE_LANE
    w_main = w_in[:, :n_qkv + dv].astype(BF16)
    w_b = w_in[:, n_qkv + dv:n_qkv + dv + v_heads]
    w_a = w_in[:, n_qkv + dv + v_heads:]
    pad = GATE_LANE - v_heads
    w_ba = jnp.pad(jnp.concatenate([jnp.pad(w_b, ((0, 0), (0, pad))), w_a], axis=1),
                   ((0, 0), (0, LANES - GATE_LANE - v_heads))).astype(BF16)
    conv_full = jnp.pad(conv_w, ((0, 0), (0, dv)))
    scale_row = jnp.concatenate([jnp.full((dk,), HEAD_DIM ** -0.5, F32), jnp.ones((dk + 2 * dv,), F32)])
    proj, ba = _gdn_in(x, _row(gain), w_main, conv_full, _row(scale_row), w_ba, 2 * dk, n_qkv, seq)
    alog_row = jnp.pad(_row(a_log), ((0, 0), (GATE_LANE, LANES - GATE_LANE - v_heads)))
    dtb_row = jnp.pad(_row(dt_bias), ((0, 0), (GATE_LANE, LANES - GATE_LANE - v_heads)))
    gcol, grow = _gdn_gates(ba, alog_row, dtb_row, batch, seq)
    o = _gdn(proj, gcol, grow, _row(out_norm), batch, seq, k_heads, v_heads)
    return _out_proj(o, w_out.astype(BF16), x)


def _conv_ffn(x, gain, w_gate, w_up, conv_w, conv_b, w_down, seq):
    h = _ffn_in(x, _row(gain), w_gate.astype(BF16), w_up.astype(BF16), conv_w, _row(conv_b), seq)
    return _out_proj(h, w_down.astype(BF16), x)


def kernel(x, norm_mix, norm_ffn, ffn_w_gate, ffn_w_up, ffn_conv_w, ffn_conv_b, ffn_w_down, a_w_in, a_b_in, a_v_norm, a_w_s, a_b_s, a_w_out, b_w_in, b_b_f, b_q_norm, b_k_norm, b_w_out, c_w_in, c_conv_w, c_a_log, c_dt_bias, c_out_norm, c_w_out):
    batch, seq, d = x.shape
    depth = norm_mix.shape[0]
    xt = x.reshape(batch * seq, d)
    for i in range(depth):
        kind, j = i % 3, i // 3
        if kind == 0:
            xt = _mixer_gmlp(xt, norm_mix[i], a_w_in[j], a_b_in[j], a_v_norm[j], a_w_s[j], a_b_s[j], a_w_out[j])
        elif kind == 1:
            xt = _mixer_fox(xt, norm_mix[i], b_w_in[j], b_b_f[j], b_q_norm[j], b_k_norm[j], b_w_out[j], batch, seq)
        else:
            xt = _mixer_gdn(xt, norm_mix[i], c_w_in[j], c_conv_w[j], c_a_log[j], c_dt_bias[j], c_out_norm[j],
                            c_w_out[j], batch, seq)
        xt = _conv_ffn(xt, norm_ffn[i], ffn_w_gate[i], ffn_w_up[i], ffn_conv_w[i], ffn_conv_b[i], ffn_w_down[i], seq)
    return xt.reshape(batch, seq, d)
```

```python
import functools

import jax
import jax.numpy as jnp
from jax import lax
from jax.experimental import pallas as pl
from jax.experimental.pallas import tpu as pltpu

RMS_EPS = 1e-6
HEAD_DIM = 128
GMLP_CHUNK = 128
GDN_CHUNK = 128
LANES = 128
HALO = 16
GATE_LANE = 32

TILE_M = 1024
TILE_N = 2048
SUB_TILE_N = 512
OUT_TILE_N = 512
RESIDENT_WEIGHT_BYTES = 8 * 1024 * 1024
ATTN_TILE = 512
GMLP_TILE_M = 256
GDN_K_HEADS_PER_STEP = 4
GDN_CHUNKS_PER_ITER = 2
GDN_SEQ_PARTS = 2
VMEM_LIMIT_BYTES = 56 * 1024 * 1024

F32 = jnp.float32
BF16 = jnp.bfloat16


def _params(*semantics, flags=None):
    return pltpu.CompilerParams(dimension_semantics=semantics, vmem_limit_bytes=VMEM_LIMIT_BYTES, flags=flags)


def _dot(a, b):
    return jnp.dot(a, b, preferred_element_type=F32)


def _dot_nt(a, b):
    return lax.dot_general(a, b, (((1,), (1,)), ((), ())), preferred_element_type=F32)


def _dot_tn(a, b):
    return lax.dot_general(a, b, (((0,), (0,)), ((), ())), preferred_element_type=F32)


def _rms_rows(xf, gain):
    ms = jnp.mean(xf * xf, axis=-1, keepdims=True)
    return xf * lax.rsqrt(ms + RMS_EPS) * gain


def _sigmoid(x):
    return 1.0 / (1.0 + jnp.exp(-x))


def _silu(x):
    return x * _sigmoid(x)


def _softplus(x):
    return jnp.maximum(x, 0.0) + jnp.log(1.0 + jnp.exp(-jnp.abs(x)))


def _gelu_tanh(x):
    return 0.5 * x * (1.0 + jnp.tanh(0.7978845608028654 * (x + 0.044715 * (x * x * x))))


def _split3(x):
    hi = x.astype(BF16)
    r1 = x - hi.astype(F32)
    mid = r1.astype(BF16)
    lo = (r1 - mid.astype(F32)).astype(BF16)
    return hi, mid, lo


def _lane_select(block, idx):
    lane = lax.broadcasted_iota(jnp.int32, block.shape, 1)
    return jnp.sum(jnp.where(lane == idx, block, 0.0), axis=1, keepdims=True)


def _causal_conv_rows(acc, prev, w, rows_ref=None):
    taps = w.shape[0]
    tm = acc.shape[0]
    out = acc * w[taps - 1:taps, :]
    if rows_ref is None:
        top = jnp.concatenate([prev[HALO - 8:, :], acc[:8, :]], axis=0)
    else:
        rows_ref[0:8, :] = prev[HALO - 8:, :]
        rows_ref[8:8 + tm, :] = acc
    for s in range(1, taps):
        if rows_ref is None:
            rolled_top = pltpu.roll(top, s, 0)[8:16, :]
            shifted = jnp.concatenate([rolled_top, pltpu.roll(acc, s, 0)[8:, :]], axis=0)
        else:
            shifted = rows_ref[8 - s:8 - s + tm, :]
        out = out + shifted * w[taps - 1 - s:taps - s, :]
    return out


def _store_normed(x_ref, g_ref, xn_ref):
    xn_ref[...] = _rms_rows(x_ref[...], g_ref[...]).astype(BF16)


def _col_chunks(tn):
    width = SUB_TILE_N if tn % SUB_TILE_N == 0 else LANES
    return [slice(c, c + width) for c in range(0, tn, width)]


def _head_slices(sl):
    return [slice(c, c + HEAD_DIM) for c in range(sl.start, sl.stop, HEAD_DIM)]


def _one_ahead(tn, matmuls, epilogue):
    chunks = _col_chunks(tn)
    pending = matmuls(chunks[0])
    for c, sl in enumerate(chunks):
        ready = pending
        if c + 1 < len(chunks):
            pending = matmuls(chunks[c + 1])
        epilogue(sl, *ready)


def _gmlp_in_kernel(x_ref, g_ref, w_ref, b_ref, o_ref, xn_ref):
    @pl.when(pl.program_id(1) == 0)
    def _():
        _store_normed(x_ref, g_ref, xn_ref)

    def finish(sl, acc):
        o_ref[:, sl] = _gelu_tanh(acc + b_ref[:, sl]).astype(o_ref.dtype)

    _one_ahead(o_ref.shape[1], lambda sl: (_dot(xn_ref[...], w_ref[:, sl]),), finish)


def _fox_in_kernel(x_ref, g_ref, w_ref, gain_ref, w2_ref, o_ref, o2_ref, xn_ref, *, n_norm_tiles):
    j = pl.program_id(1)

    @pl.when(j == 0)
    def _():
        _store_normed(x_ref, g_ref, xn_ref)
        o2_ref[...] = _dot(xn_ref[...], w2_ref[...])

    def matmul(sl):
        return (_dot(xn_ref[...], w_ref[:, sl]),)

    def head_norm(sl, acc):
        for k, hs in enumerate(_head_slices(sl)):
            blk = acc[:, k * HEAD_DIM:(k + 1) * HEAD_DIM]
            o_ref[:, hs] = _rms_rows(blk, gain_ref[:, hs]).astype(o_ref.dtype)

    def plain(sl, acc):
        o_ref[:, sl] = acc.astype(o_ref.dtype)

    @pl.when(j < n_norm_tiles)
    def _():
        _one_ahead(o_ref.shape[1], matmul, head_norm)

    @pl.when(j >= n_norm_tiles)
    def _():
        _one_ahead(o_ref.shape[1], matmul, plain)


def _seq_start_mask(prev, tiles_per_seq):
    first = (pl.program_id(0) % tiles_per_seq) == 0
    return jnp.where(first, 0.0, prev)


def _gdn_in_kernel(x_ref, xp_ref, g_ref, w_ref, cw_ref, scale_ref, w2_ref, o_ref, o2_ref, xn_ref, xnp_ref,
                   rows_ref, *, n_qk_tiles, n_conv_tiles, tiles_per_seq):
    j = pl.program_id(1)

    @pl.when(j == 0)
    def _():
        _store_normed(x_ref, g_ref, xn_ref)
        _store_normed(xp_ref, g_ref, xnp_ref)
        o2_ref[...] = _dot(xn_ref[...], w2_ref[...])

    def matmul(sl):
        return (_dot(xn_ref[...], w_ref[:, sl]),)

    def matmul_with_halo(sl):
        return matmul(sl) + (_seq_start_mask(_dot(xnp_ref[...], w_ref[:, sl]), tiles_per_seq),)

    def conv_silu_l2norm(sl, acc, prev):
        y = _silu(_causal_conv_rows(acc, prev, cw_ref[:, sl], rows_ref))
        for k, hs in enumerate(_head_slices(sl)):
            blk = y[:, k * HEAD_DIM:(k + 1) * HEAD_DIM]
            inv = lax.rsqrt(jnp.sum(blk * blk, axis=-1, keepdims=True) + RMS_EPS)
            o_ref[:, hs] = (blk * inv * scale_ref[:, hs]).astype(o_ref.dtype)

    def conv_silu(sl, acc, prev):
        o_ref[:, sl] = _silu(_causal_conv_rows(acc, prev, cw_ref[:, sl], rows_ref)).astype(o_ref.dtype)

    def plain(sl, acc):
        o_ref[:, sl] = acc.astype(o_ref.dtype)

    @pl.when(j < n_qk_tiles)
    def _():
        _one_ahead(o_ref.shape[1], matmul_with_halo, conv_silu_l2norm)

    @pl.when(jnp.logical_and(j >= n_qk_tiles, j < n_conv_tiles))
    def _():
        _one_ahead(o_ref.shape[1], matmul_with_halo, conv_silu)

    @pl.when(j >= n_conv_tiles)
    def _():
        _one_ahead(o_ref.shape[1], matmul, plain)


def _ffn_in_kernel(x_ref, xp_ref, g_ref, wg_ref, wu_ref, cw_ref, cb_ref, o_ref, xn_ref, xnp_ref, *, tiles_per_seq):
    @pl.when(pl.program_id(1) == 0)
    def _():
        _store_normed(x_ref, g_ref, xn_ref)
        _store_normed(xp_ref, g_ref, xnp_ref)

    for sl in _col_chunks(o_ref.shape[1]):
        gate = _dot(xn_ref[...], wg_ref[:, sl])
        up = _dot(xn_ref[...], wu_ref[:, sl])
        prev = _seq_start_mask(_dot(xnp_ref[...], wg_ref[:, sl]), tiles_per_seq)
        gate = _causal_conv_rows(gate, prev, cw_ref[:, sl]) + cb_ref[:, sl]
        o_ref[:, sl] = (_silu(gate) * up).astype(o_ref.dtype)


def _row_tile(t):
    return min(TILE_M, t)


def _col_tile(n, align, limit=TILE_N):
    tn = min(limit, n)
    while n % tn or align % tn:
        tn -= LANES
    return tn


def _x_specs(tm, d, halo):
    specs = [pl.BlockSpec((tm, d), lambda i, j: (i, 0))]
    if halo:
        per = tm // HALO
        specs.append(pl.BlockSpec((HALO, d), lambda i, j: (jnp.maximum(i * per - 1, 0), 0)))
    return specs


def _col(rows, tn):
    return pl.BlockSpec((rows, tn), lambda i, j: (0, j))


def _conv_rows_scratch(tm, tn):
    chunk = _col_chunks(tn)[0]
    return pltpu.VMEM((8 + tm, chunk.stop - chunk.start), F32)


def _gmlp_in(x, gain, w, b):
    t, d = x.shape
    n = w.shape[1]
    tm, tn = _row_tile(t), _col_tile(n, n)
    return pl.pallas_call(
        _gmlp_in_kernel,
        grid=(t // tm, n // tn),
        in_specs=_x_specs(tm, d, False) + [pl.BlockSpec((1, d), lambda i, j: (0, 0)), _col(d, tn), _col(1, tn)],
        out_specs=pl.BlockSpec((tm, tn), lambda i, j: (i, j)),
        out_shape=jax.ShapeDtypeStruct((t, n), BF16),
        scratch_shapes=[pltpu.VMEM((tm, d), BF16)],
        compiler_params=_params("parallel", "arbitrary"),
        name="gmlp_in",
    )(x, gain, w, b)


def _fox_in(x, gain, w, head_gain, w2, n_norm_cols):
    t, d = x.shape
    n = w.shape[1]
    tm, tn = _row_tile(t), _col_tile(n, n_norm_cols)
    kern = functools.partial(_fox_in_kernel, n_norm_tiles=n_norm_cols // tn)
    return pl.pallas_call(
        kern,
        grid=(t // tm, n // tn),
        in_specs=_x_specs(tm, d, False) + [pl.BlockSpec((1, d), lambda i, j: (0, 0)), _col(d, tn), _col(1, tn),
                                           pl.BlockSpec((d, LANES), lambda i, j: (0, 0))],
        out_specs=[pl.BlockSpec((tm, tn), lambda i, j: (i, j)), pl.BlockSpec((tm, LANES), lambda i, j: (i, 0))],
        out_shape=[jax.ShapeDtypeStruct((t, n), BF16), jax.ShapeDtypeStruct((t, LANES), F32)],
        scratch_shapes=[pltpu.VMEM((tm, d), BF16)],
        compiler_params=_params("parallel", "arbitrary"),
        name="fox_in",
    )(x, gain, w, head_gain, w2)


def _gdn_in(x, gain, w, conv_w, scale_row, w2, n_qk_cols, n_conv_cols, seq):
    t, d = x.shape
    n = w.shape[1]
    tm, tn = _row_tile(min(t, seq)), _col_tile(n, n_qk_cols)
    kern = functools.partial(_gdn_in_kernel, n_qk_tiles=n_qk_cols // tn, n_conv_tiles=n_conv_cols // tn,
                             tiles_per_seq=seq // tm)
    return pl.pallas_call(
        kern,
        grid=(t // tm, n // tn),
        in_specs=_x_specs(tm, d, True) + [pl.BlockSpec((1, d), lambda i, j: (0, 0)), _col(d, tn),
                                          _col(conv_w.shape[0], tn), _col(1, tn),
                                          pl.BlockSpec((d, LANES), lambda i, j: (0, 0))],
        out_specs=[pl.BlockSpec((tm, tn), lambda i, j: (i, j)), pl.BlockSpec((tm, LANES), lambda i, j: (i, 0))],
        out_shape=[jax.ShapeDtypeStruct((t, n), BF16), jax.ShapeDtypeStruct((t, LANES), F32)],
        scratch_shapes=[pltpu.VMEM((tm, d), BF16), pltpu.VMEM((HALO, d), BF16), _conv_rows_scratch(tm, tn)],
        compiler_params=_params("parallel", "arbitrary"),
        name="gdn_in",
    )(x, x, gain, w, conv_w, scale_row, w2)


def _ffn_in(x, gain, wg, wu, conv_w, conv_b, seq):
    t, d = x.shape
    n = wg.shape[1]
    tm, tn = _row_tile(min(t, seq)), _col_tile(n, n, SUB_TILE_N)
    kern = functools.partial(_ffn_in_kernel, tiles_per_seq=seq // tm)
    return pl.pallas_call(
        kern,
        grid=(t // tm, n // tn),
        in_specs=_x_specs(tm, d, True) + [pl.BlockSpec((1, d), lambda i, j: (0, 0)), _col(d, tn), _col(d, tn),
                                          _col(conv_w.shape[0], tn), _col(1, tn)],
        out_specs=pl.BlockSpec((tm, tn), lambda i, j: (i, j)),
        out_shape=jax.ShapeDtypeStruct((t, n), BF16),
        scratch_shapes=[pltpu.VMEM((tm, d), BF16), pltpu.VMEM((HALO, d), BF16)],
        compiler_params=_params("parallel", "arbitrary"),
        name="ffn_in",
    )(x, x, gain, wg, wu, conv_w, conv_b)


def _out_proj_kernel(a_ref, w_ref, x_ref, o_ref):
    for sl in _col_chunks(o_ref.shape[1]):
        o_ref[:, sl] = x_ref[:, sl] + _dot(a_ref[...], w_ref[:, sl])


def _out_proj(a, w, x):
    t, k = a.shape
    d = w.shape[1]
    resident = k * d * w.dtype.itemsize <= RESIDENT_WEIGHT_BYTES
    tm, tn = _row_tile(t), d if resident else min(OUT_TILE_N, d)
    w_spec = (pl.BlockSpec((k, tn), lambda i, j: (0, 0), pipeline_mode=pl.Buffered(1)) if resident
              else pl.BlockSpec((k, tn), lambda i, j: (0, j)))
    return pl.pallas_call(
        _out_proj_kernel,
        grid=(t // tm, d // tn),
        in_specs=[pl.BlockSpec((tm, k), lambda i, j: (i, 0)), w_spec,
                  pl.BlockSpec((tm, tn), lambda i, j: (i, j))],
        out_specs=pl.BlockSpec((tm, tn), lambda i, j: (i, j)),
        out_shape=jax.ShapeDtypeStruct((t, d), F32),
        compiler_params=_params("parallel", "arbitrary"),
        name="out_proj",
    )(a, w, x)


def _gmlp_gate_kernel(u_ref, v_ref, vg_ref, ws_ref, bst_ref, o_ref, vn_ref):
    vn_ref[...] = _rms_rows(v_ref[...].astype(F32), vg_ref[...]).astype(BF16)
    tm = u_ref.shape[0]
    groups = ws_ref.shape[0]
    row = lax.broadcasted_iota(jnp.int32, (GMLP_CHUNK, GMLP_CHUNK), 0)
    col = lax.broadcasted_iota(jnp.int32, (GMLP_CHUNK, GMLP_CHUNK), 1)
    for g in range(groups):
        cs = slice(g * HEAD_DIM, (g + 1) * HEAD_DIM)
        w_causal = jnp.where(row >= col, ws_ref[g], 0.0).astype(BF16)
        bias = bst_ref[:, g:g + 1]
        for c in range(tm // GMLP_CHUNK):
            rs = slice(c * GMLP_CHUNK, (c + 1) * GMLP_CHUNK)
            sv = _dot(w_causal, vn_ref[rs, cs]) + bias
            o_ref[rs, cs] = (u_ref[rs, cs].astype(F32) * sv).astype(o_ref.dtype)


def _gmlp_gate(h, v_gain, w_s, b_s_t):
    t, two_a = h.shape
    a = two_a // 2
    tm = min(GMLP_TILE_M, t)
    groups = w_s.shape[0]
    return pl.pallas_call(
        _gmlp_gate_kernel,
        grid=(t // tm,),
        in_specs=[pl.BlockSpec((tm, a), lambda i: (i, 0)), pl.BlockSpec((tm, a), lambda i: (i, 1)),
                  pl.BlockSpec((1, a), lambda i: (0, 0)),
                  pl.BlockSpec((groups, GMLP_CHUNK, GMLP_CHUNK), lambda i: (0, 0, 0)),
                  pl.BlockSpec((GMLP_CHUNK, groups), lambda i: (0, 0))],
        out_specs=pl.BlockSpec((tm, a), lambda i: (i, 0)),
        out_shape=jax.ShapeDtypeStruct((t, a), BF16),
        scratch_shapes=[pltpu.VMEM((tm, a), BF16)],
        compiler_params=_params("parallel"),
        name="gmlp_gate",
    )(h, h, v_gain, w_s, b_s_t)


def _tri_ones(n):
    row = lax.broadcasted_iota(jnp.int32, (n, n), 0)
    col = lax.broadcasted_iota(jnp.int32, (n, n), 1)
    return jnp.where(row >= col, 1.0, 0.0).astype(BF16)


def _cumsum_rows(tri, x):
    hi, mid, lo = _split3(x)
    return _dot(tri, hi) + _dot(tri, mid) + _dot(tri, lo)


def _fox_decay_kernel(fl_ref, bf_ref, ccol_ref, crow_ref):
    tk = crow_ref.shape[2]
    tri = _tri_ones(tk)
    carry = jnp.zeros((1, LANES), F32)
    for n in range(crow_ref.shape[0]):
        rs = slice(n * tk, (n + 1) * tk)
        log_f = -_softplus(-(fl_ref[rs, :] + bf_ref[...]))
        c = _cumsum_rows(tri, log_f) + carry
        carry = c[tk - 1:tk, :]
        ccol_ref[rs, :] = c
        crow_ref[n] = c.T


def _fox_decay(fl, b_f_row, batch, seq, tk):
    nkv = seq // tk
    return pl.pallas_call(
        _fox_decay_kernel,
        grid=(batch,),
        in_specs=[pl.BlockSpec((seq, LANES), lambda b: (b, 0)), pl.BlockSpec((1, LANES), lambda b: (0, 0))],
        out_specs=[pl.BlockSpec((seq, LANES), lambda b: (b, 0)),
                   pl.BlockSpec((None, nkv, LANES, tk), lambda b: (b, 0, 0, 0))],
        out_shape=[jax.ShapeDtypeStruct((batch * seq, LANES), F32),
                   jax.ShapeDtypeStruct((batch, nkv, LANES, tk), F32)],
        compiler_params=_params("parallel"),
        name="fox_decay",
    )(fl, b_f_row)


def _attn_chains(nq, n_chains=2):
    chains = [[] for _ in range(n_chains)]
    for i in reversed(range(nq)):
        min(chains, key=len).extend((i, j) for j in range(i + 1))
    return chains


def _fox_attn_kernel(q_ref, k_ref, v_ref, og_ref, ccol_ref, crow_ref, o_ref, *, tile, scale):
    h = pl.program_id(1)
    nq = q_ref.shape[0] // tile
    log2e = 1.4426950408889634
    row_in_tile = h % 8
    row = lax.broadcasted_iota(jnp.int32, (tile, tile), 0)
    col = lax.broadcasted_iota(jnp.int32, (tile, tile), 1)
    chains = _attn_chains(nq)
    state = [None] * len(chains)

    def rows(i):
        return pl.ds(i * tile, tile)

    for step in range(max(len(c) for c in chains)):
        live = [(n, c[step]) for n, c in enumerate(chains) if step < len(c)]
        scores = [_dot_nt(q_ref[rows(i), :], k_ref[rows(j), :]) for _, (i, j) in live]
        probs = []
        for (n, (i, j)), s in zip(live, scores):
            c_t = _lane_select(ccol_ref[rows(i), :], h) * log2e
            c_s = crow_ref[j, pl.ds(row_in_tile, 1), :] * log2e
            t = s * (scale * log2e) - c_s
            if i == j:
                t = jnp.where(row >= col, t, -1e30)
            m_new = jnp.max(t, axis=1, keepdims=True) + c_t
            if j > 0:
                m_old, l_old, acc_old = state[n]
                m_new = jnp.maximum(m_old, m_new)
            p = jnp.exp2(t - (m_new - c_t))
            l_new = jnp.sum(p, axis=1, keepdims=True)
            if j > 0:
                alpha = jnp.exp2(m_old - m_new)
                l_new = alpha * l_old + l_new
                state[n] = (m_new, l_new, alpha * acc_old)
            else:
                state[n] = (m_new, l_new, None)
            probs.append(p.astype(BF16))
        updates = [_dot(p, v_ref[rows(j), :]) for (_, (_, j)), p in zip(live, probs)]
        for (n, (i, j)), pv in zip(live, updates):
            m, l, acc = state[n]
            acc = pv if acc is None else acc + pv
            state[n] = (m, l, acc)
            if i == j:
                gate = _sigmoid(og_ref[rows(i), :].astype(F32))
                o_ref[rows(i), :] = (acc / l * gate).astype(o_ref.dtype)


def _fox_attn(proj, ccol, crow, batch, seq, heads):
    t = proj.shape[0]
    tile = crow.shape[3]
    nq = seq // tile
    kern = functools.partial(_fox_attn_kernel, tile=tile, scale=HEAD_DIM ** -0.5)

    def head_block(offset):
        return pl.BlockSpec((seq, HEAD_DIM), lambda b, h: (b, offset + h))

    return pl.pallas_call(
        kern,
        grid=(batch, heads),
        in_specs=[head_block(0), head_block(heads), head_block(2 * heads), head_block(3 * heads),
                  pl.BlockSpec((seq, LANES), lambda b, h: (b, 0)),
                  pl.BlockSpec((None, nq, 8, tile), lambda b, h: (b, 0, h // 8, 0))],
        out_specs=head_block(0),
        out_shape=jax.ShapeDtypeStruct((t, heads * HEAD_DIM), BF16),
        compiler_params=_params("parallel", "arbitrary"),
        name="fox_attn",
    )(proj, proj, proj, proj, ccol, crow)


def _gdn_gates_kernel(ba_ref, alog_ref, dtb_ref, gcol_ref, grow_ref):
    chunk = grow_ref.shape[2]
    tri = _tri_ones(chunk)
    lane = lax.broadcasted_iota(jnp.int32, (chunk, LANES), 1)
    for n in range(grow_ref.shape[0]):
        rs = slice(n * chunk, (n + 1) * chunk)
        raw = ba_ref[rs, :]
        beta = _sigmoid(raw)
        g = -jnp.exp(alog_ref[...]) * _softplus(raw + dtb_ref[...])
        packed = jnp.where(lane < GATE_LANE, beta, _cumsum_rows(tri, g))
        gcol_ref[rs, :] = packed
        grow_ref[n] = packed.T


def _gdn_gates(ba, alog_row, dtb_row, batch, seq):
    nchunks = seq // GDN_CHUNK
    return pl.pallas_call(
        _gdn_gates_kernel,
        grid=(batch,),
        in_specs=[pl.BlockSpec((seq, LANES), lambda b: (b, 0)), pl.BlockSpec((1, LANES), lambda b: (0, 0)),
                  pl.BlockSpec((1, LANES), lambda b: (0, 0))],
        out_specs=[pl.BlockSpec((seq, LANES), lambda b: (b, 0)),
                   pl.BlockSpec((None, nchunks, LANES, GDN_CHUNK), lambda b: (b, 0, 0, 0))],
        out_shape=[jax.ShapeDtypeStruct((batch * seq, LANES), F32),
                   jax.ShapeDtypeStruct((batch, nchunks, LANES, GDN_CHUNK), F32)],
        compiler_params=_params("parallel"),
        name="gdn_gates",
    )(ba, alog_row, dtb_row)


def _block_diag(pack):
    n = pack.shape[0]
    reps = pack.shape[1] // n
    zero = jnp.zeros((n, n), pack.dtype)
    rows = []
    for r in range(reps):
        blk = pack[:, r * n:(r + 1) * n]
        rows.append(jnp.concatenate([blk if c == r else zero for c in range(reps)], axis=1))
    return jnp.concatenate(rows, axis=0)


def _unit_lower_inverses(mats):
    n = mats[0].shape[0]
    row = lax.broadcasted_iota(jnp.int32, mats[0].shape, 0)
    col = lax.broadcasted_iota(jnp.int32, mats[0].shape, 1)
    eye = jnp.where(row == col % n, 1.0, 0.0)
    bs = [-a for a in mats]
    b_his = [b.astype(BF16) for b in bs]
    ps = [eye + b for b in bs]
    ms = [_dot(b_hi, _block_diag(b_hi)) for b_hi in b_his]
    power = 2
    while power < n:
        mbs = [m.astype(BF16) for m in ms]
        if 2 * power < n:
            both = [_dot(jnp.concatenate([p.astype(BF16), mb], axis=0), _block_diag(mb)) for p, mb in zip(ps, mbs)]
            ps = [p + x[:n] for p, x in zip(ps, both)]
            ms = [x[n:] for x in both]
        else:
            ps = [p + _dot(p.astype(BF16), _block_diag(mb)) for p, mb in zip(ps, mbs)]
        power *= 2
    b_los = [(b - b_hi.astype(F32)).astype(BF16) for b, b_hi in zip(bs, b_his)]
    p_his = [p.astype(BF16) for p in ps]
    p_los = [(p - p_hi.astype(F32)).astype(BF16) for p, p_hi in zip(ps, p_his)]
    bps = [_dot(jnp.concatenate([b_hi, b_lo], axis=0), _block_diag(p_hi))
           for b_hi, b_lo, p_hi in zip(b_his, b_los, p_his)]
    cross = [_dot(b_hi, _block_diag(p_lo)) for b_hi, p_lo in zip(b_his, p_los)]
    resids = [(eye - p) + (bp[:n] + (bp[n:] + x)) for p, bp, x in zip(ps, bps, cross)]
    return [p + _dot(p_hi, _block_diag(r.astype(BF16))) for p, p_hi, r in zip(ps, p_his, resids)]


def _gdn_kernel(q_ref, k_ref, v_ref, z_ref, gcol_ref, grow_ref, onorm_ref, o_ref,
                u_ref, w_ref, attn_ref, state_ref, *, k_per_step, rep, chunks_per_iter):
    step = pl.program_id(1)
    nchunks, chunk, _ = q_ref.shape
    row = lax.broadcasted_iota(jnp.int32, (chunk, chunk), 0)
    col = lax.broadcasted_iota(jnp.int32, (chunk, chunk), 1)

    def head_slice(idx):
        return slice(idx * HEAD_DIM, (idx + 1) * HEAD_DIM)

    def pass1(i, _):
        jobs = [(i * chunks_per_iter + c, j) for c in range(chunks_per_iter) for j in range(k_per_step)]
        qs = [q_ref[n, :, head_slice(j)] for n, j in jobs]
        ks = [k_ref[n, :, head_slice(j)] for n, j in jobs]
        kks = [_dot_nt(k, k) for k in ks]
        qks = [_dot_nt(q, k) for q, k in zip(qs, ks)]
        a_packs, rhs = [], []
        for (n, j), k, kk, qk in zip(jobs, ks, kks, qks):
            gates = gcol_ref[n]
            kf = k.astype(F32)
            a_blocks, attn_blocks, job_rhs = [], [], []
            for r in range(rep):
                hv = (step * k_per_step + j) * rep + r
                beta = _lane_select(gates, hv)
                gc = _lane_select(gates, GATE_LANE + hv)
                gc_row = grow_ref[n, pl.ds(GATE_LANE + hv, 1), :]
                decay = jnp.exp(jnp.where(row >= col, gc - gc_row, -1e30))
                a_blocks.append(jnp.where(row > col, beta * kk * decay, 0.0))
                attn_blocks.append((qk * decay).astype(BF16))
                v = v_ref[n, :, head_slice(j * rep + r)].astype(F32)
                job_rhs.append(jnp.concatenate([(v * beta).astype(BF16),
                                                (kf * (beta * jnp.exp(gc))).astype(BF16)], axis=1))
            attn_ref[n, j] = jnp.concatenate(attn_blocks, axis=1)
            a_packs.append(jnp.concatenate(a_blocks, axis=1))
            rhs.append(job_rhs)
        t_invs = [t.astype(BF16) for t in _unit_lower_inverses(a_packs)]
        sols = [[_dot(t_inv[:, head_slice(r)], job_rhs[r]) for r in range(rep)]
                for t_inv, job_rhs in zip(t_invs, rhs)]
        for (n, j), job_sols in zip(jobs, sols):
            u_ref[n, j] = jnp.concatenate([s[:, :HEAD_DIM] for s in job_sols], axis=1)
            w_ref[n, j] = jnp.concatenate([s[:, HEAD_DIM:] for s in job_sols], axis=1).astype(BF16)
        return 0

    def pass2(n, _):
        gates = gcol_ref[n]
        heads = range(k_per_step)
        qfs = [q_ref[n, :, head_slice(j)].astype(F32) for j in heads]
        kfs = [k_ref[n, :, head_slice(j)].astype(F32) for j in heads]
        gcs = [[_lane_select(gates, GATE_LANE + (step * k_per_step + j) * rep + r) for r in range(rep)] for j in heads]
        states = [[state_ref[j, r] for r in range(rep)] for j in heads]
        q_decs = [jnp.concatenate([(qfs[j] * jnp.exp(gc)).astype(BF16) for gc in gcs[j]], axis=1) for j in heads]
        s_bds = [_block_diag(jnp.concatenate([s.astype(BF16) for s in states[j]], axis=1)) for j in heads]
        boths = [_dot(jnp.concatenate([w_ref[n, j], q_decs[j]], axis=0), s_bds[j]) for j in heads]
        vbs = [(u_ref[n, j] - boths[j][:chunk]).astype(BF16) for j in heads]
        g_lasts = [[gc[chunk - 1:chunk, :] for gc in gcs[j]] for j in heads]
        k_decs = [[(kfs[j] * jnp.exp(g_lasts[j][r] - gcs[j][r])).astype(BF16) for r in range(rep)] for j in heads]
        updates = [[_dot_tn(k_decs[j][r], vbs[j][:, head_slice(r)]) for r in range(rep)] for j in heads]
        for j in heads:
            for r in range(rep):
                state_ref[j, r] = states[j][r] * jnp.exp(g_lasts[j][r]) + updates[j][r]
        outs = [boths[j][chunk:] + _dot(attn_ref[n, j], _block_diag(vbs[j])) for j in heads]
        for j in heads:
            for r in range(rep):
                hs = head_slice(j * rep + r)
                zf = z_ref[n, :, hs].astype(F32)
                o_ref[n, :, hs] = (_rms_rows(outs[j][:, head_slice(r)], onorm_ref[...]) * _silu(zf)).astype(o_ref.dtype)
        return 0

    lax.fori_loop(0, nchunks // chunks_per_iter, pass1, 0)

    @pl.when(pl.program_id(2) == 0)
    def _():
        state_ref[...] = jnp.zeros_like(state_ref)

    lax.fori_loop(0, nchunks, pass2, 0)


def _gdn(proj, gcol, grow, out_norm, batch, seq, k_heads, v_heads):
    t = proj.shape[0]
    nchunks = seq // GDN_CHUNK
    rep = v_heads // k_heads
    kps = min(GDN_K_HEADS_PER_STEP, k_heads)
    steps = k_heads // kps
    parts = GDN_SEQ_PARTS if nchunks % GDN_SEQ_PARTS == 0 else 1
    nc = nchunks // parts
    proj3 = proj.reshape(batch * nchunks, GDN_CHUNK, proj.shape[1])
    gcol3 = gcol.reshape(batch * nchunks, GDN_CHUNK, LANES)
    kw, vw = kps * HEAD_DIM, kps * rep * HEAD_DIM
    v_base = 2 * k_heads * HEAD_DIM // vw
    z_base = v_base + v_heads * HEAD_DIM // vw
    kern = functools.partial(_gdn_kernel, k_per_step=kps, rep=rep, chunks_per_iter=min(GDN_CHUNKS_PER_ITER, nc))
    pack = rep * HEAD_DIM

    def cols(width, offset):
        return pl.BlockSpec((nc, GDN_CHUNK, width), lambda b, h, s: (b * parts + s, 0, offset + h))

    out = pl.pallas_call(
        kern,
        grid=(batch, steps, parts),
        in_specs=[cols(kw, 0), cols(kw, steps), cols(vw, v_base), cols(vw, z_base),
                  pl.BlockSpec((nc, GDN_CHUNK, LANES), lambda b, h, s: (b * parts + s, 0, 0)),
                  pl.BlockSpec((None, nc, LANES, GDN_CHUNK), lambda b, h, s: (b, s, 0, 0)),
                  pl.BlockSpec((1, HEAD_DIM), lambda b, h, s: (0, 0))],
        out_specs=cols(vw, 0),
        out_shape=jax.ShapeDtypeStruct((batch * nchunks, GDN_CHUNK, v_heads * HEAD_DIM), BF16),
        scratch_shapes=[pltpu.VMEM((nc, kps, GDN_CHUNK, pack), F32),
                        pltpu.VMEM((nc, kps, GDN_CHUNK, pack), BF16),
                        pltpu.VMEM((nc, kps, GDN_CHUNK, pack), BF16),
                        pltpu.VMEM((kps, rep, HEAD_DIM, HEAD_DIM), F32)],
        compiler_params=_params("parallel", "arbitrary", "arbitrary"),
        name="gdn_delta_rule",
    )(proj3, proj3, proj3, proj3, gcol3, grow, out_norm)
    return out.reshape(t, v_heads * HEAD_DIM)


def _row(v, width=None):
    v = v.reshape(1, -1).astype(F32)
    if width is not None and v.shape[1] < width:
        v = jnp.pad(v, ((0, 0), (0, width - v.shape[1])))
    return v


def _mixer_gmlp(x, gain, w_in, b_in, v_norm, w_s, b_s, w_out):
    h = _gmlp_in(x, _row(gain), w_in.astype(BF16), _row(b_in))
    gated = _gmlp_gate(h, _row(v_norm), w_s, b_s.T)
    return _out_proj(gated, w_out.astype(BF16), x)


def _mixer_fox(x, gain, w_in, b_f, q_norm, k_norm, w_out, batch, seq):
    heads = b_f.shape[0]
    dim = heads * HEAD_DIM
    w_main = w_in[:, :4 * dim].astype(BF16)
    w_f = jnp.pad(w_in[:, 4 * dim:], ((0, 0), (0, LANES - heads))).astype(BF16)
    head_gain = jnp.concatenate([jnp.tile(q_norm, heads), jnp.tile(k_norm, heads), jnp.ones((2 * dim,), F32)])
    proj, fl = _fox_in(x, _row(gain), w_main, _row(head_gain), w_f, 2 * dim)
    tq = min(ATTN_TILE, seq)
    ccol, crow = _fox_decay(fl, _row(b_f, LANES), batch, seq, tq)
    o = _fox_attn(proj, ccol, crow, batch, seq, heads)
    return _out_proj(o, w_out.astype(BF16), x)


def _mixer_gdn(x, gain, w_in, conv_w, a_log, dt_bias, out_norm, w_out, batch, seq):
    v_heads = a_log.shape[0]
    dv = v_heads * HEAD_DIM
    n_qkv = conv_w.shape[1]
    dk = (n_qkv - dv) // 2
    k_heads = dk // HEAD_DIM
    assert v_heads <= GATE_LANE
    w_main = w_in[:, :n_qkv + dv].astype(BF16)
    w_b = w_in[:, n_qkv + dv:n_qkv + dv + v_heads]
    w_a = w_in[:, n_qkv + dv + v_heads:]
    pad = GATE_LANE - v_heads
    w_ba = jnp.pad(jnp.concatenate([jnp.pad(w_b, ((0, 0), (0, pad))), w_a], axis=1),
                   ((0, 0), (0, LANES - GATE_LANE - v_heads))).astype(BF16)
    conv_full = jnp.pad(conv_w, ((0, 0), (0, dv)))
    scale_row = jnp.concatenate([jnp.full((dk,), HEAD_DIM ** -0.5, F32), jnp.ones((dk + 2 * dv,), F32)])
    proj, ba = _gdn_in(x, _row(gain), w_main, conv_full, _row(scale_row), w_ba, 2 * dk, n_qkv, seq)
    alog_row = jnp.pad(_row(a_log), ((0, 0), (GATE_LANE, LANES - GATE_LANE - v_heads)))
    dtb_row = jnp.pad(_row(dt_bias), ((0, 0), (GATE_LANE, LANES - GATE_LANE - v_heads)))
    gcol, grow = _gdn_gates(ba, alog_row, dtb_row, batch, seq)
    o = _gdn(proj, gcol, grow, _row(out_norm), batch, seq, k_heads, v_heads)
    return _out_proj(o, w_out.astype(BF16), x)


def _conv_ffn(x, gain, w_gate, w_up, conv_w, conv_b, w_down, seq):
    h = _ffn_in(x, _row(gain), w_gate.astype(BF16), w_up.astype(BF16), conv_w, _row(conv_b), seq)
    return _out_proj(h, w_down.astype(BF16), x)


def kernel(x, norm_mix, norm_ffn, ffn_w_gate, ffn_w_up, ffn_conv_w, ffn_conv_b, ffn_w_down, a_w_in, a_b_in, a_v_norm, a_w_s, a_b_s, a_w_out, b_w_in, b_b_f, b_q_norm, b_k_norm, b_w_out, c_w_in, c_conv_w, c_a_log, c_dt_bias, c_out_norm, c_w_out):
    batch, seq, d = x.shape
    depth = norm_mix.shape[0]
    xt = x.reshape(batch * seq, d)
    for i in range(depth):
        kind, j = i % 3, i // 3
        if kind == 0:
            xt = _mixer_gmlp(xt, norm_mix[i], a_w_in[j], a_b_in[j], a_v_norm[j], a_w_s[j], a_b_s[j], a_w_out[j])
        elif kind == 1:
            xt = _mixer_fox(xt, norm_mix[i], b_w_in[j], b_b_f[j], b_q_norm[j], b_k_norm[j], b_w_out[j], batch, seq)
        else:
            xt = _mixer_gdn(xt, norm_mix[i], c_w_in[j], c_conv_w[j], c_a_log[j], c_dt_bias[j], c_out_norm[j],
                            c_w_out[j], batch, seq)
        xt = _conv_ffn(xt, norm_ffn[i], ffn_w_gate[i], ffn_w_up[i], ffn_conv_w[i], ffn_conv_b[i], ffn_w_down[i], seq)
    return xt.reshape(batch, seq, d)
```

```python
import functools

import jax
import jax.numpy as jnp
from jax import lax
from jax.experimental import pallas as pl
from jax.experimental.pallas import tpu as pltpu

RMS_EPS = 1e-6
HEAD_DIM = 128
GMLP_CHUNK = 128
GDN_CHUNK = 128
LANES = 128
HALO = 16
GATE_LANE = 32

TILE_M = 1024
TILE_N = 2048
SUB_TILE_N = 512
OUT_TILE_N = 512
RESIDENT_WEIGHT_BYTES = 8 * 1024 * 1024
ATTN_TILE = 512
GMLP_TILE_M = 256
GDN_K_HEADS_PER_STEP = 4
GDN_CHUNKS_PER_ITER = 2
GDN_SEQ_PARTS = 2
VMEM_LIMIT_BYTES = 56 * 1024 * 1024

F32 = jnp.float32
BF16 = jnp.bfloat16


def _params(*semantics, flags=None):
    return pltpu.CompilerParams(dimension_semantics=semantics, vmem_limit_bytes=VMEM_LIMIT_BYTES, flags=flags)


def _dot(a, b):
    return jnp.dot(a, b, preferred_element_type=F32)


def _dot_nt(a, b):
    return lax.dot_general(a, b, (((1,), (1,)), ((), ())), preferred_element_type=F32)


def _dot_tn(a, b):
    return lax.dot_general(a, b, (((0,), (0,)), ((), ())), preferred_element_type=F32)


def _rms_rows(xf, gain):
    ms = jnp.mean(xf * xf, axis=-1, keepdims=True)
    return xf * lax.rsqrt(ms + RMS_EPS) * gain


def _sigmoid(x):
    return 1.0 / (1.0 + jnp.exp(-x))


def _silu(x):
    return x * _sigmoid(x)


def _softplus(x):
    return jnp.maximum(x, 0.0) + jnp.log(1.0 + jnp.exp(-jnp.abs(x)))


def _gelu_tanh(x):
    return 0.5 * x * (1.0 + jnp.tanh(0.7978845608028654 * (x + 0.044715 * (x * x * x))))


def _split3(x):
    hi = x.astype(BF16)
    r1 = x - hi.astype(F32)
    mid = r1.astype(BF16)
    lo = (r1 - mid.astype(F32)).astype(BF16)
    return hi, mid, lo


def _lane_select(block, idx):
    lane = lax.broadcasted_iota(jnp.int32, block.shape, 1)
    return jnp.sum(jnp.where(lane == idx, block, 0.0), axis=1, keepdims=True)


def _causal_conv_rows(acc, prev, w, rows_ref=None):
    taps = w.shape[0]
    tm = acc.shape[0]
    out = acc * w[taps - 1:taps, :]
    if rows_ref is None:
        top = jnp.concatenate([prev[HALO - 8:, :], acc[:8, :]], axis=0)
    else:
        rows_ref[0:8, :] = prev[HALO - 8:, :]
        rows_ref[8:8 + tm, :] = acc
    for s in range(1, taps):
        if rows_ref is None:
            rolled_top = pltpu.roll(top, s, 0)[8:16, :]
            shifted = jnp.concatenate([rolled_top, pltpu.roll(acc, s, 0)[8:, :]], axis=0)
        else:
            shifted = rows_ref[8 - s:8 - s + tm, :]
        out = out + shifted * w[taps - 1 - s:taps - s, :]
    return out


def _store_normed(x_ref, g_ref, xn_ref):
    xn_ref[...] = _rms_rows(x_ref[...], g_ref[...]).astype(BF16)


def _col_chunks(tn):
    width = SUB_TILE_N if tn % SUB_TILE_N == 0 else LANES
    return [slice(c, c + width) for c in range(0, tn, width)]


def _head_slices(sl):
    return [slice(c, c + HEAD_DIM) for c in range(sl.start, sl.stop, HEAD_DIM)]


def _one_ahead(tn, matmuls, epilogue):
    chunks = _col_chunks(tn)
    pending = matmuls(chunks[0])
    for c, sl in enumerate(chunks):
        ready = pending
        if c + 1 < len(chunks):
            pending = matmuls(chunks[c + 1])
        epilogue(sl, *ready)


def _gmlp_in_kernel(x_ref, g_ref, w_ref, b_ref, o_ref, xn_ref):
    @pl.when(pl.program_id(1) == 0)
    def _():
        _store_normed(x_ref, g_ref, xn_ref)

    def finish(sl, acc):
        o_ref[:, sl] = _gelu_tanh(acc + b_ref[:, sl]).astype(o_ref.dtype)

    _one_ahead(o_ref.shape[1], lambda sl: (_dot(xn_ref[...], w_ref[:, sl]),), finish)


def _fox_in_kernel(x_ref, g_ref, w_ref, gain_ref, w2_ref, o_ref, o2_ref, xn_ref, *, n_norm_tiles):
    j = pl.program_id(1)

    @pl.when(j == 0)
    def _():
        _store_normed(x_ref, g_ref, xn_ref)
        o2_ref[...] = _dot(xn_ref[...], w2_ref[...])

    def matmul(sl):
        return (_dot(xn_ref[...], w_ref[:, sl]),)

    def head_norm(sl, acc):
        for k, hs in enumerate(_head_slices(sl)):
            blk = acc[:, k * HEAD_DIM:(k + 1) * HEAD_DIM]
            o_ref[:, hs] = _rms_rows(blk, gain_ref[:, hs]).astype(o_ref.dtype)

    def plain(sl, acc):
        o_ref[:, sl] = acc.astype(o_ref.dtype)

    @pl.when(j < n_norm_tiles)
    def _():
        _one_ahead(o_ref.shape[1], matmul, head_norm)

    @pl.when(j >= n_norm_tiles)
    def _():
        _one_ahead(o_ref.shape[1], matmul, plain)


def _seq_start_mask(prev, tiles_per_seq):
    first = (pl.program_id(0) % tiles_per_seq) == 0
    return jnp.where(first, 0.0, prev)


def _gdn_in_kernel(x_ref, xp_ref, g_ref, w_ref, cw_ref, scale_ref, w2_ref, o_ref, o2_ref, xn_ref, xnp_ref,
                   rows_ref, *, n_qk_tiles, n_conv_tiles, tiles_per_seq):
    j = pl.program_id(1)

    @pl.when(j == 0)
    def _():
        _store_normed(x_ref, g_ref, xn_ref)
        _store_normed(xp_ref, g_ref, xnp_ref)
        o2_ref[...] = _dot(xn_ref[...], w2_ref[...])

    def matmul(sl):
        return (_dot(xn_ref[...], w_ref[:, sl]),)

    def matmul_with_halo(sl):
        return matmul(sl) + (_seq_start_mask(_dot(xnp_ref[...], w_ref[:, sl]), tiles_per_seq),)

    def conv_silu_l2norm(sl, acc, prev):
        y = _silu(_causal_conv_rows(acc, prev, cw_ref[:, sl], rows_ref))
        for k, hs in enumerate(_head_slices(sl)):
            blk = y[:, k * HEAD_DIM:(k + 1) * HEAD_DIM]
            inv = lax.rsqrt(jnp.sum(blk * blk, axis=-1, keepdims=True) + RMS_EPS)
            o_ref[:, hs] = (blk * inv * scale_ref[:, hs]).astype(o_ref.dtype)

    def conv_silu(sl, acc, prev):
        o_ref[:, sl] = _silu(_causal_conv_rows(acc, prev, cw_ref[:, sl], rows_ref)).astype(o_ref.dtype)

    def plain(sl, acc):
        o_ref[:, sl] = acc.astype(o_ref.dtype)

    @pl.when(j < n_qk_tiles)
    def _():
        _one_ahead(o_ref.shape[1], matmul_with_halo, conv_silu_l2norm)

    @pl.when(jnp.logical_and(j >= n_qk_tiles, j < n_conv_tiles))
    def _():
        _one_ahead(o_ref.shape[1], matmul_with_halo, conv_silu)

    @pl.when(j >= n_conv_tiles)
    def _():
        _one_ahead(o_ref.shape[1], matmul, plain)


def _ffn_in_kernel(x_ref, xp_ref, g_ref, wg_ref, wu_ref, cw_ref, cb_ref, o_ref, xn_ref, xnp_ref, *, tiles_per_seq):
    @pl.when(pl.program_id(1) == 0)
    def _():
        _store_normed(x_ref, g_ref, xn_ref)
        _store_normed(xp_ref, g_ref, xnp_ref)

    for sl in _col_chunks(o_ref.shape[1]):
        gate = _dot(xn_ref[...], wg_ref[:, sl])
        up = _dot(xn_ref[...], wu_ref[:, sl])
        prev = _seq_start_mask(_dot(xnp_ref[...], wg_ref[:, sl]), tiles_per_seq)
        gate = _causal_conv_rows(gate, prev, cw_ref[:, sl]) + cb_ref[:, sl]
        o_ref[:, sl] = (_silu(gate) * up).astype(o_ref.dtype)


def _row_tile(t):
    return min(TILE_M, t)


def _col_tile(n, align, limit=TILE_N):
    tn = min(limit, n)
    while n % tn or align % tn:
        tn -= LANES
    return tn


def _x_specs(tm, d, halo):
    specs = [pl.BlockSpec((tm, d), lambda i, j: (i, 0))]
    if halo:
        per = tm // HALO
        specs.append(pl.BlockSpec((HALO, d), lambda i, j: (jnp.maximum(i * per - 1, 0), 0)))
    return specs


def _col(rows, tn):
    return pl.BlockSpec((rows, tn), lambda i, j: (0, j))


def _conv_rows_scratch(tm, tn):
    chunk = _col_chunks(tn)[0]
    return pltpu.VMEM((8 + tm, chunk.stop - chunk.start), F32)


def _gmlp_in(x, gain, w, b):
    t, d = x.shape
    n = w.shape[1]
    tm, tn = _row_tile(t), _col_tile(n, n)
    return pl.pallas_call(
        _gmlp_in_kernel,
        grid=(t // tm, n // tn),
        in_specs=_x_specs(tm, d, False) + [pl.BlockSpec((1, d), lambda i, j: (0, 0)), _col(d, tn), _col(1, tn)],
        out_specs=pl.BlockSpec((tm, tn), lambda i, j: (i, j)),
        out_shape=jax.ShapeDtypeStruct((t, n), BF16),
        scratch_shapes=[pltpu.VMEM((tm, d), BF16)],
        compiler_params=_params("parallel", "arbitrary"),
        name="gmlp_in",
    )(x, gain, w, b)


def _fox_in(x, gain, w, head_gain, w2, n, n_norm_cols):
    t, d = x.shape
    tm, tn = _row_tile(t), _col_tile(n, n_norm_cols)
    kern = functools.partial(_fox_in_kernel, n_norm_tiles=n_norm_cols // tn)
    return pl.pallas_call(
        kern,
        grid=(t // tm, n // tn),
        in_specs=_x_specs(tm, d, False) + [pl.BlockSpec((1, d), lambda i, j: (0, 0)), _col(d, tn), _col(1, tn),
                                           pl.BlockSpec((d, LANES), lambda i, j: (0, 0))],
        out_specs=[pl.BlockSpec((tm, tn), lambda i, j: (i, j)), pl.BlockSpec((tm, LANES), lambda i, j: (i, 0))],
        out_shape=[jax.ShapeDtypeStruct((t, n), BF16), jax.ShapeDtypeStruct((t, LANES), F32)],
        scratch_shapes=[pltpu.VMEM((tm, d), BF16)],
        compiler_params=_params("parallel", "arbitrary"),
        name="fox_in",
    )(x, gain, w, head_gain, w2)


def _gdn_in(x, gain, w, conv_w, scale_row, w2, n, n_qk_cols, n_conv_cols, seq):
    t, d = x.shape
    tm, tn = _row_tile(min(t, seq)), _col_tile(n, n_qk_cols)
    kern = functools.partial(_gdn_in_kernel, n_qk_tiles=n_qk_cols // tn, n_conv_tiles=n_conv_cols // tn,
                             tiles_per_seq=seq // tm)
    return pl.pallas_call(
        kern,
        grid=(t // tm, n // tn),
        in_specs=_x_specs(tm, d, True) + [pl.BlockSpec((1, d), lambda i, j: (0, 0)), _col(d, tn),
                                          _col(conv_w.shape[0], tn), _col(1, tn),
                                          pl.BlockSpec((d, LANES), lambda i, j: (0, 0))],
        out_specs=[pl.BlockSpec((tm, tn), lambda i, j: (i, j)), pl.BlockSpec((tm, LANES), lambda i, j: (i, 0))],
        out_shape=[jax.ShapeDtypeStruct((t, n), BF16), jax.ShapeDtypeStruct((t, LANES), F32)],
        scratch_shapes=[pltpu.VMEM((tm, d), BF16), pltpu.VMEM((HALO, d), BF16), _conv_rows_scratch(tm, tn)],
        compiler_params=_params("parallel", "arbitrary"),
        name="gdn_in",
    )(x, x, gain, w, conv_w, scale_row, w2)


def _ffn_in(x, gain, wg, wu, conv_w, conv_b, seq):
    t, d = x.shape
    n = wg.shape[1]
    tm, tn = _row_tile(min(t, seq)), _col_tile(n, n, SUB_TILE_N)
    kern = functools.partial(_ffn_in_kernel, tiles_per_seq=seq // tm)
    return pl.pallas_call(
        kern,
        grid=(t // tm, n // tn),
        in_specs=_x_specs(tm, d, True) + [pl.BlockSpec((1, d), lambda i, j: (0, 0)), _col(d, tn), _col(d, tn),
                                          _col(conv_w.shape[0], tn), _col(1, tn)],
        out_specs=pl.BlockSpec((tm, tn), lambda i, j: (i, j)),
        out_shape=jax.ShapeDtypeStruct((t, n), BF16),
        scratch_shapes=[pltpu.VMEM((tm, d), BF16), pltpu.VMEM((HALO, d), BF16)],
        compiler_params=_params("parallel", "arbitrary"),
        name="ffn_in",
    )(x, x, gain, wg, wu, conv_w, conv_b)


def _out_proj_kernel(a_ref, w_ref, x_ref, o_ref):
    for sl in _col_chunks(o_ref.shape[1]):
        o_ref[:, sl] = x_ref[:, sl] + _dot(a_ref[...], w_ref[:, sl])


def _out_proj(a, w, x):
    t, k = a.shape
    d = w.shape[1]
    resident = k * d * w.dtype.itemsize <= RESIDENT_WEIGHT_BYTES
    tm, tn = _row_tile(t), d if resident else min(OUT_TILE_N, d)
    w_spec = (pl.BlockSpec((k, tn), lambda i, j: (0, 0), pipeline_mode=pl.Buffered(1)) if resident
              else pl.BlockSpec((k, tn), lambda i, j: (0, j)))
    return pl.pallas_call(
        _out_proj_kernel,
        grid=(t // tm, d // tn),
        in_specs=[pl.BlockSpec((tm, k), lambda i, j: (i, 0)), w_spec,
                  pl.BlockSpec((tm, tn), lambda i, j: (i, j))],
        out_specs=pl.BlockSpec((tm, tn), lambda i, j: (i, j)),
        out_shape=jax.ShapeDtypeStruct((t, d), F32),
        compiler_params=_params("parallel", "arbitrary"),
        name="out_proj",
    )(a, w, x)


def _gmlp_gate_kernel(u_ref, v_ref, vg_ref, ws_ref, bst_ref, o_ref, vn_ref):
    vn_ref[...] = _rms_rows(v_ref[...].astype(F32), vg_ref[...]).astype(BF16)
    tm = u_ref.shape[0]
    groups = ws_ref.shape[0]
    row = lax.broadcasted_iota(jnp.int32, (GMLP_CHUNK, GMLP_CHUNK), 0)
    col = lax.broadcasted_iota(jnp.int32, (GMLP_CHUNK, GMLP_CHUNK), 1)
    for g in range(groups):
        cs = slice(g * HEAD_DIM, (g + 1) * HEAD_DIM)
        w_causal = jnp.where(row >= col, ws_ref[g], 0.0).astype(BF16)
        bias = bst_ref[:, g:g + 1]
        for c in range(tm // GMLP_CHUNK):
            rs = slice(c * GMLP_CHUNK, (c + 1) * GMLP_CHUNK)
            sv = _dot(w_causal, vn_ref[rs, cs]) + bias
            o_ref[rs, cs] = (u_ref[rs, cs].astype(F32) * sv).astype(o_ref.dtype)


def _gmlp_gate(h, v_gain, w_s, b_s_t):
    t, two_a = h.shape
    a = two_a // 2
    tm = min(GMLP_TILE_M, t)
    groups = w_s.shape[0]
    return pl.pallas_call(
        _gmlp_gate_kernel,
        grid=(t // tm,),
        in_specs=[pl.BlockSpec((tm, a), lambda i: (i, 0)), pl.BlockSpec((tm, a), lambda i: (i, 1)),
                  pl.BlockSpec((1, a), lambda i: (0, 0)),
                  pl.BlockSpec((groups, GMLP_CHUNK, GMLP_CHUNK), lambda i: (0, 0, 0)),
                  pl.BlockSpec((GMLP_CHUNK, groups), lambda i: (0, 0))],
        out_specs=pl.BlockSpec((tm, a), lambda i: (i, 0)),
        out_shape=jax.ShapeDtypeStruct((t, a), BF16),
        scratch_shapes=[pltpu.VMEM((tm, a), BF16)],
        compiler_params=_params("parallel"),
        name="gmlp_gate",
    )(h, h, v_gain, w_s, b_s_t)


def _tri_ones(n):
    row = lax.broadcasted_iota(jnp.int32, (n, n), 0)
    col = lax.broadcasted_iota(jnp.int32, (n, n), 1)
    return jnp.where(row >= col, 1.0, 0.0).astype(BF16)


def _cumsum_rows(tri, x):
    hi, mid, lo = _split3(x)
    return _dot(tri, hi) + _dot(tri, mid) + _dot(tri, lo)


def _fox_decay_kernel(fl_ref, bf_ref, ccol_ref, crow_ref):
    tk = crow_ref.shape[2]
    tri = _tri_ones(tk)
    carry = jnp.zeros((1, LANES), F32)
    for n in range(crow_ref.shape[0]):
        rs = slice(n * tk, (n + 1) * tk)
        log_f = -_softplus(-(fl_ref[rs, :] + bf_ref[...]))
        c = _cumsum_rows(tri, log_f) + carry
        carry = c[tk - 1:tk, :]
        ccol_ref[rs, :] = c
        crow_ref[n] = c.T


def _fox_decay(fl, b_f_row, batch, seq, tk):
    nkv = seq // tk
    return pl.pallas_call(
        _fox_decay_kernel,
        grid=(batch,),
        in_specs=[pl.BlockSpec((seq, LANES), lambda b: (b, 0)), pl.BlockSpec((1, LANES), lambda b: (0, 0))],
        out_specs=[pl.BlockSpec((seq, LANES), lambda b: (b, 0)),
                   pl.BlockSpec((None, nkv, LANES, tk), lambda b: (b, 0, 0, 0))],
        out_shape=[jax.ShapeDtypeStruct((batch * seq, LANES), F32),
                   jax.ShapeDtypeStruct((batch, nkv, LANES, tk), F32)],
        compiler_params=_params("parallel"),
        name="fox_decay",
    )(fl, b_f_row)


def _attn_chains(nq, n_chains=2):
    chains = [[] for _ in range(n_chains)]
    for i in reversed(range(nq)):
        min(chains, key=len).extend((i, j) for j in range(i + 1))
    return chains


def _fox_attn_kernel(q_ref, k_ref, v_ref, og_ref, ccol_ref, crow_ref, o_ref, *, tile, scale):
    h = pl.program_id(1)
    nq = q_ref.shape[0] // tile
    log2e = 1.4426950408889634
    row_in_tile = h % 8
    row = lax.broadcasted_iota(jnp.int32, (tile, tile), 0)
    col = lax.broadcasted_iota(jnp.int32, (tile, tile), 1)
    chains = _attn_chains(nq)
    state = [None] * len(chains)

    def rows(i):
        return pl.ds(i * tile, tile)

    for step in range(max(len(c) for c in chains)):
        live = [(n, c[step]) for n, c in enumerate(chains) if step < len(c)]
        scores = [_dot_nt(q_ref[rows(i), :], k_ref[rows(j), :]) for _, (i, j) in live]
        probs = []
        for (n, (i, j)), s in zip(live, scores):
            c_t = _lane_select(ccol_ref[rows(i), :], h) * log2e
            c_s = crow_ref[j, pl.ds(row_in_tile, 1), :] * log2e
            t = s * (scale * log2e) - c_s
            if i == j:
                t = jnp.where(row >= col, t, -1e30)
            m_new = jnp.max(t, axis=1, keepdims=True) + c_t
            if j > 0:
                m_old, l_old, acc_old = state[n]
                m_new = jnp.maximum(m_old, m_new)
            p = jnp.exp2(t - (m_new - c_t))
            l_new = jnp.sum(p, axis=1, keepdims=True)
            if j > 0:
                alpha = jnp.exp2(m_old - m_new)
                l_new = alpha * l_old + l_new
                state[n] = (m_new, l_new, alpha * acc_old)
            else:
                state[n] = (m_new, l_new, None)
            probs.append(p.astype(BF16))
        updates = [_dot(p, v_ref[rows(j), :]) for (_, (_, j)), p in zip(live, probs)]
        for (n, (i, j)), pv in zip(live, updates):
            m, l, acc = state[n]
            acc = pv if acc is None else acc + pv
            state[n] = (m, l, acc)
            if i == j:
                gate = _sigmoid(og_ref[rows(i), :].astype(F32))
                o_ref[rows(i), :] = (acc / l * gate).astype(o_ref.dtype)


def _fox_attn(proj, ccol, crow, batch, seq, heads):
    t = proj.shape[0]
    tile = crow.shape[3]
    nq = seq // tile
    kern = functools.partial(_fox_attn_kernel, tile=tile, scale=HEAD_DIM ** -0.5)

    def head_block(offset):
        return pl.BlockSpec((seq, HEAD_DIM), lambda b, h: (b, offset + h))

    return pl.pallas_call(
        kern,
        grid=(batch, heads),
        in_specs=[head_block(0), head_block(heads), head_block(2 * heads), head_block(3 * heads),
                  pl.BlockSpec((seq, LANES), lambda b, h: (b, 0)),
                  pl.BlockSpec((None, nq, 8, tile), lambda b, h: (b, 0, h // 8, 0))],
        out_specs=head_block(0),
        out_shape=jax.ShapeDtypeStruct((t, heads * HEAD_DIM), BF16),
        compiler_params=_params("parallel", "arbitrary"),
        name="fox_attn",
    )(proj, proj, proj, proj, ccol, crow)


def _gdn_gates_kernel(ba_ref, alog_ref, dtb_ref, gcol_ref, grow_ref):
    chunk = grow_ref.shape[2]
    tri = _tri_ones(chunk)
    lane = lax.broadcasted_iota(jnp.int32, (chunk, LANES), 1)
    for n in range(grow_ref.shape[0]):
        rs = slice(n * chunk, (n + 1) * chunk)
        raw = ba_ref[rs, :]
        beta = _sigmoid(raw)
        g = -jnp.exp(alog_ref[...]) * _softplus(raw + dtb_ref[...])
        packed = jnp.where(lane < GATE_LANE, beta, _cumsum_rows(tri, g))
        gcol_ref[rs, :] = packed
        grow_ref[n] = packed.T


def _gdn_gates(ba, alog_row, dtb_row, batch, seq):
    nchunks = seq // GDN_CHUNK
    return pl.pallas_call(
        _gdn_gates_kernel,
        grid=(batch,),
        in_specs=[pl.BlockSpec((seq, LANES), lambda b: (b, 0)), pl.BlockSpec((1, LANES), lambda b: (0, 0)),
                  pl.BlockSpec((1, LANES), lambda b: (0, 0))],
        out_specs=[pl.BlockSpec((seq, LANES), lambda b: (b, 0)),
                   pl.BlockSpec((None, nchunks, LANES, GDN_CHUNK), lambda b: (b, 0, 0, 0))],
        out_shape=[jax.ShapeDtypeStruct((batch * seq, LANES), F32),
                   jax.ShapeDtypeStruct((batch, nchunks, LANES, GDN_CHUNK), F32)],
        compiler_params=_params("parallel"),
        name="gdn_gates",
    )(ba, alog_row, dtb_row)


def _block_diag(pack):
    n = pack.shape[0]
    reps = pack.shape[1] // n
    zero = jnp.zeros((n, n), pack.dtype)
    rows = []
    for r in range(reps):
        blk = pack[:, r * n:(r + 1) * n]
        rows.append(jnp.concatenate([blk if c == r else zero for c in range(reps)], axis=1))
    return jnp.concatenate(rows, axis=0)


def _unit_lower_inverses(mats):
    n = mats[0].shape[0]
    row = lax.broadcasted_iota(jnp.int32, mats[0].shape, 0)
    col = lax.broadcasted_iota(jnp.int32, mats[0].shape, 1)
    eye = jnp.where(row == col % n, 1.0, 0.0)
    bs = [-a for a in mats]
    b_his = [b.astype(BF16) for b in bs]
    ps = [eye + b for b in bs]
    ms = [_dot(b_hi, _block_diag(b_hi)) for b_hi in b_his]
    power = 2
    while power < n:
        mbs = [m.astype(BF16) for m in ms]
        if 2 * power < n:
            both = [_dot(jnp.concatenate([p.astype(BF16), mb], axis=0), _block_diag(mb)) for p, mb in zip(ps, mbs)]
            ps = [p + x[:n] for p, x in zip(ps, both)]
            ms = [x[n:] for x in both]
        else:
            ps = [p + _dot(p.astype(BF16), _block_diag(mb)) for p, mb in zip(ps, mbs)]
        power *= 2
    b_los = [(b - b_hi.astype(F32)).astype(BF16) for b, b_hi in zip(bs, b_his)]
    p_his = [p.astype(BF16) for p in ps]
    p_los = [(p - p_hi.astype(F32)).astype(BF16) for p, p_hi in zip(ps, p_his)]
    bps = [_dot(jnp.concatenate([b_hi, b_lo], axis=0), _block_diag(p_hi))
           for b_hi, b_lo, p_hi in zip(b_his, b_los, p_his)]
    cross = [_dot(b_hi, _block_diag(p_lo)) for b_hi, p_lo in zip(b_his, p_los)]
    resids = [(eye - p) + (bp[:n] + (bp[n:] + x)) for p, bp, x in zip(ps, bps, cross)]
    return [p + _dot(p_hi, _block_diag(r.astype(BF16))) for p, p_hi, r in zip(ps, p_his, resids)]


def _gdn_kernel(q_ref, k_ref, v_ref, z_ref, gcol_ref, grow_ref, onorm_ref, o_ref,
                u_ref, w_ref, attn_ref, state_ref, *, k_per_step, rep, chunks_per_iter):
    step = pl.program_id(1)
    nchunks, chunk, _ = q_ref.shape
    row = lax.broadcasted_iota(jnp.int32, (chunk, chunk), 0)
    col = lax.broadcasted_iota(jnp.int32, (chunk, chunk), 1)

    def head_slice(idx):
        return slice(idx * HEAD_DIM, (idx + 1) * HEAD_DIM)

    def pass1(i, _):
        jobs = [(i * chunks_per_iter + c, j) for c in range(chunks_per_iter) for j in range(k_per_step)]
        qs = [q_ref[n, :, head_slice(j)] for n, j in jobs]
        ks = [k_ref[n, :, head_slice(j)] for n, j in jobs]
        kks = [_dot_nt(k, k) for k in ks]
        qks = [_dot_nt(q, k) for q, k in zip(qs, ks)]
        a_packs, rhs = [], []
        for (n, j), k, kk, qk in zip(jobs, ks, kks, qks):
            gates = gcol_ref[n]
            kf = k.astype(F32)
            a_blocks, attn_blocks, job_rhs = [], [], []
            for r in range(rep):
                hv = (step * k_per_step + j) * rep + r
                beta = _lane_select(gates, hv)
                gc = _lane_select(gates, GATE_LANE + hv)
                gc_row = grow_ref[n, pl.ds(GATE_LANE + hv, 1), :]
                decay = jnp.exp(jnp.where(row >= col, gc - gc_row, -1e30))
                a_blocks.append(jnp.where(row > col, beta * kk * decay, 0.0))
                attn_blocks.append((qk * decay).astype(BF16))
                v = v_ref[n, :, head_slice(j * rep + r)].astype(F32)
                job_rhs.append(jnp.concatenate([(v * beta).astype(BF16),
                                                (kf * (beta * jnp.exp(gc))).astype(BF16)], axis=1))
            attn_ref[n, j] = jnp.concatenate(attn_blocks, axis=1)
            a_packs.append(jnp.concatenate(a_blocks, axis=1))
            rhs.append(job_rhs)
        t_invs = [t.astype(BF16) for t in _unit_lower_inverses(a_packs)]
        sols = [[_dot(t_inv[:, head_slice(r)], job_rhs[r]) for r in range(rep)]
                for t_inv, job_rhs in zip(t_invs, rhs)]
        for (n, j), job_sols in zip(jobs, sols):
            u_ref[n, j] = jnp.concatenate([s[:, :HEAD_DIM] for s in job_sols], axis=1)
            w_ref[n, j] = jnp.concatenate([s[:, HEAD_DIM:] for s in job_sols], axis=1).astype(BF16)
        return 0

    def pass2(n, _):
        gates = gcol_ref[n]
        heads = range(k_per_step)
        qfs = [q_ref[n, :, head_slice(j)].astype(F32) for j in heads]
        kfs = [k_ref[n, :, head_slice(j)].astype(F32) for j in heads]
        gcs = [[_lane_select(gates, GATE_LANE + (step * k_per_step + j) * rep + r) for r in range(rep)] for j in heads]
        states = [[state_ref[j, r] for r in range(rep)] for j in heads]
        q_decs = [jnp.concatenate([(qfs[j] * jnp.exp(gc)).astype(BF16) for gc in gcs[j]], axis=1) for j in heads]
        s_bds = [_block_diag(jnp.concatenate([s.astype(BF16) for s in states[j]], axis=1)) for j in heads]
        boths = [_dot(jnp.concatenate([w_ref[n, j], q_decs[j]], axis=0), s_bds[j]) for j in heads]
        vbs = [(u_ref[n, j] - boths[j][:chunk]).astype(BF16) for j in heads]
        g_lasts = [[gc[chunk - 1:chunk, :] for gc in gcs[j]] for j in heads]
        k_decs = [[(kfs[j] * jnp.exp(g_lasts[j][r] - gcs[j][r])).astype(BF16) for r in range(rep)] for j in heads]
        updates = [[_dot_tn(k_decs[j][r], vbs[j][:, head_slice(r)]) for r in range(rep)] for j in heads]
        for j in heads:
            for r in range(rep):
                state_ref[j, r] = states[j][r] * jnp.exp(g_lasts[j][r]) + updates[j][r]
        outs = [boths[j][chunk:] + _dot(attn_ref[n, j], _block_diag(vbs[j])) for j in heads]
        for j in heads:
            for r in range(rep):
                hs = head_slice(j * rep + r)
                zf = z_ref[n, :, hs].astype(F32)
                o_ref[n, :, hs] = (_rms_rows(outs[j][:, head_slice(r)], onorm_ref[...]) * _silu(zf)).astype(o_ref.dtype)
        return 0

    lax.fori_loop(0, nchunks // chunks_per_iter, pass1, 0)

    @pl.when(pl.program_id(2) == 0)
    def _():
        state_ref[...] = jnp.zeros_like(state_ref)

    lax.fori_loop(0, nchunks, pass2, 0)


def _gdn(proj, gcol, grow, out_norm, batch, seq, k_heads, v_heads):
    t = proj.shape[0]
    nchunks = seq // GDN_CHUNK
    rep = v_heads // k_heads
    kps = min(GDN_K_HEADS_PER_STEP, k_heads)
    steps = k_heads // kps
    parts = GDN_SEQ_PARTS if nchunks % GDN_SEQ_PARTS == 0 else 1
    nc = nchunks // parts
    proj3 = proj.reshape(batch * nchunks, GDN_CHUNK, proj.shape[1])
    gcol3 = gcol.reshape(batch * nchunks, GDN_CHUNK, LANES)
    kw, vw = kps * HEAD_DIM, kps * rep * HEAD_DIM
    v_base = 2 * k_heads * HEAD_DIM // vw
    z_base = v_base + v_heads * HEAD_DIM // vw
    kern = functools.partial(_gdn_kernel, k_per_step=kps, rep=rep, chunks_per_iter=min(GDN_CHUNKS_PER_ITER, nc))
    pack = rep * HEAD_DIM

    def cols(width, offset):
        return pl.BlockSpec((nc, GDN_CHUNK, width), lambda b, h, s: (b * parts + s, 0, offset + h))

    out = pl.pallas_call(
        kern,
        grid=(batch, steps, parts),
        in_specs=[cols(kw, 0), cols(kw, steps), cols(vw, v_base), cols(vw, z_base),
                  pl.BlockSpec((nc, GDN_CHUNK, LANES), lambda b, h, s: (b * parts + s, 0, 0)),
                  pl.BlockSpec((None, nc, LANES, GDN_CHUNK), lambda b, h, s: (b, s, 0, 0)),
                  pl.BlockSpec((1, HEAD_DIM), lambda b, h, s: (0, 0))],
        out_specs=cols(vw, 0),
        out_shape=jax.ShapeDtypeStruct((batch * nchunks, GDN_CHUNK, v_heads * HEAD_DIM), BF16),
        scratch_shapes=[pltpu.VMEM((nc, kps, GDN_CHUNK, pack), F32),
                        pltpu.VMEM((nc, kps, GDN_CHUNK, pack), BF16),
                        pltpu.VMEM((nc, kps, GDN_CHUNK, pack), BF16),
                        pltpu.VMEM((kps, rep, HEAD_DIM, HEAD_DIM), F32)],
        compiler_params=_params("parallel", "arbitrary", "arbitrary"),
        name="gdn_delta_rule",
    )(proj3, proj3, proj3, proj3, gcol3, grow, out_norm)
    return out.reshape(t, v_heads * HEAD_DIM)


def _row(v, width=None):
    v = v.reshape(1, -1).astype(F32)
    if width is not None and v.shape[1] < width:
        v = jnp.pad(v, ((0, 0), (0, width - v.shape[1])))
    return v


def _mixer_gmlp(x, gain, w_in, b_in, v_norm, w_s, b_s, w_out):
    h = _gmlp_in(x, _row(gain), w_in.astype(BF16), _row(b_in))
    gated = _gmlp_gate(h, _row(v_norm), w_s, b_s.T)
    return _out_proj(gated, w_out.astype(BF16), x)


def _mixer_fox(x, gain, w_in, b_f, q_norm, k_norm, w_out, batch, seq):
    heads = b_f.shape[0]
    dim = heads * HEAD_DIM
    w_f = jnp.pad(w_in[:, 4 * dim:], ((0, 0), (0, LANES - heads))).astype(BF16)
    head_gain = jnp.concatenate([jnp.tile(q_norm, heads), jnp.tile(k_norm, heads), jnp.ones((2 * dim,), F32)])
    proj, fl = _fox_in(x, _row(gain), w_in.astype(BF16), _row(head_gain), w_f, 4 * dim, 2 * dim)
    tq = min(ATTN_TILE, seq)
    ccol, crow = _fox_decay(fl, _row(b_f, LANES), batch, seq, tq)
    o = _fox_attn(proj, ccol, crow, batch, seq, heads)
    return _out_proj(o, w_out.astype(BF16), x)


def _mixer_gdn(x, gain, w_in, conv_w, a_log, dt_bias, out_norm, w_out, batch, seq):
    v_heads = a_log.shape[0]
    dv = v_heads * HEAD_DIM
    n_qkv = conv_w.shape[1]
    dk = (n_qkv - dv) // 2
    k_heads = dk // HEAD_DIM
    assert v_heads <= GATE_LANE
    w_b = w_in[:, n_qkv + dv:n_qkv + dv + v_heads]
    w_a = w_in[:, n_qkv + dv + v_heads:]
    pad = GATE_LANE - v_heads
    w_ba = jnp.pad(jnp.concatenate([jnp.pad(w_b, ((0, 0), (0, pad))), w_a], axis=1),
                   ((0, 0), (0, LANES - GATE_LANE - v_heads))).astype(BF16)
    conv_full = jnp.pad(conv_w, ((0, 0), (0, dv)))
    scale_row = jnp.concatenate([jnp.full((dk,), HEAD_DIM ** -0.5, F32), jnp.ones((dk + 2 * dv,), F32)])
    proj, ba = _gdn_in(x, _row(gain), w_in.astype(BF16), conv_full, _row(scale_row), w_ba, n_qkv + dv, 2 * dk, n_qkv, seq)
    alog_row = jnp.pad(_row(a_log), ((0, 0), (GATE_LANE, LANES - GATE_LANE - v_heads)))
    dtb_row = jnp.pad(_row(dt_bias), ((0, 0), (GATE_LANE, LANES - GATE_LANE - v_heads)))
    gcol, grow = _gdn_gates(ba, alog_row, dtb_row, batch, seq)
    o = _gdn(proj, gcol, grow, _row(out_norm), batch, seq, k_heads, v_heads)
    return _out_proj(o, w_out.astype(BF16), x)


def _conv_ffn(x, gain, w_gate, w_up, conv_w, conv_b, w_down, seq):
    h = _ffn_in(x, _row(gain), w_gate.astype(BF16), w_up.astype(BF16), conv_w, _row(conv_b), seq)
    return _out_proj(h, w_down.astype(BF16), x)


def kernel(x, norm_mix, norm_ffn, ffn_w_gate, ffn_w_up, ffn_conv_w, ffn_conv_b, ffn_w_down, a_w_in, a_b_in, a_v_norm, a_w_s, a_b_s, a_w_out, b_w_in, b_b_f, b_q_norm, b_k_norm, b_w_out, c_w_in, c_conv_w, c_a_log, c_dt_bias, c_out_norm, c_w_out):
    batch, seq, d = x.shape
    depth = norm_mix.shape[0]
    xt = x.reshape(batch * seq, d)
    for i in range(depth):
        kind, j = i % 3, i // 3
        if kind == 0:
            xt = _mixer_gmlp(xt, norm_mix[i], a_w_in[j], a_b_in[j], a_v_norm[j], a_w_s[j], a_b_s[j], a_w_out[j])
        elif kind == 1:
            xt = _mixer_fox(xt, norm_mix[i], b_w_in[j], b_b_f[j], b_q_norm[j], b_k_norm[j], b_w_out[j], batch, seq)
        else:
            xt = _mixer_gdn(xt, norm_mix[i], c_w_in[j], c_conv_w[j], c_a_log[j], c_dt_bias[j], c_out_norm[j],
                            c_w_out[j], batch, seq)
        xt = _conv_ffn(xt, norm_ffn[i], ffn_w_gate[i], ffn_w_up[i], ffn_conv_w[i], ffn_conv_b[i], ffn_w_down[i], seq)
    return xt.reshape(batch, seq, d)
```

```python
import functools

import jax
import jax.numpy as jnp
from jax import lax
from jax.experimental import pallas as pl
from jax.experimental.pallas import tpu as pltpu

RMS_EPS = 1e-6
HEAD_DIM = 128
GMLP_CHUNK = 128
GDN_CHUNK = 128
LANES = 128
SUBLANES = 8
MASKED = -1e30
HALO = 16
GATE_LANE = 32

TILE_M = 1024
TILE_N = 2048
SUB_TILE_N = 512
OUT_TILE_N = 512
RESIDENT_WEIGHT_BYTES = 8 * 1024 * 1024
ATTN_TILE = 512
GMLP_TILE_M = 256
GDN_K_HEADS_PER_STEP = 4
GDN_CHUNKS_PER_ITER = 2
GDN_SEQ_PARTS = 2
VMEM_LIMIT_BYTES = 56 * 1024 * 1024

F32 = jnp.float32
BF16 = jnp.bfloat16


def _params(*semantics, flags=None):
    return pltpu.CompilerParams(dimension_semantics=semantics, vmem_limit_bytes=VMEM_LIMIT_BYTES, flags=flags)


def _dot(a, b):
    return jnp.dot(a, b, preferred_element_type=F32)


def _dot_nt(a, b):
    return lax.dot_general(a, b, (((1,), (1,)), ((), ())), preferred_element_type=F32)


def _dot_tn(a, b):
    return lax.dot_general(a, b, (((0,), (0,)), ((), ())), preferred_element_type=F32)


def _rms_rows(xf, gain):
    ms = jnp.mean(xf * xf, axis=-1, keepdims=True)
    return xf * lax.rsqrt(ms + RMS_EPS) * gain


def _sigmoid(x):
    return 1.0 / (1.0 + jnp.exp(-x))


def _silu(x):
    half = 0.5 * x
    return half + half * jnp.tanh(half)


def _softplus(x):
    return jnp.maximum(x, 0.0) + jnp.log(1.0 + jnp.exp(-jnp.abs(x)))


def _gelu_tanh(x):
    return 0.5 * x * (1.0 + jnp.tanh(0.7978845608028654 * (x + 0.044715 * (x * x * x))))


def _split3(x):
    hi = x.astype(BF16)
    r1 = x - hi.astype(F32)
    mid = r1.astype(BF16)
    lo = (r1 - mid.astype(F32)).astype(BF16)
    return hi, mid, lo


def _lane_select(block, idx):
    lane = lax.broadcasted_iota(jnp.int32, block.shape, 1)
    return jnp.sum(jnp.where(lane == idx, block, 0.0), axis=1, keepdims=True)


def _causal_conv_rows(acc, prev, w, rows_ref=None):
    taps = w.shape[0]
    tm = acc.shape[0]
    out = acc * w[taps - 1:taps, :]
    above = prev[HALO - SUBLANES:, :]
    if rows_ref is None:
        top = jnp.concatenate([above, acc[:SUBLANES, :]], axis=0)
    else:
        rows_ref[0:SUBLANES, :] = above
        rows_ref[SUBLANES:SUBLANES + tm, :] = acc
    for s in range(1, taps):
        if rows_ref is None:
            rolled_top = pltpu.roll(top, s, 0)[SUBLANES:, :]
            shifted = jnp.concatenate([rolled_top, pltpu.roll(acc, s, 0)[SUBLANES:, :]], axis=0)
        else:
            shifted = rows_ref[SUBLANES - s:SUBLANES - s + tm, :]
        out = out + shifted * w[taps - 1 - s:taps - s, :]
    return out


def _store_normed(x_ref, g_ref, xn_ref):
    xn_ref[...] = _rms_rows(x_ref[...], g_ref[...]).astype(BF16)


def _col_chunks(tn):
    width = SUB_TILE_N if tn % SUB_TILE_N == 0 else LANES
    return [slice(c, c + width) for c in range(0, tn, width)]


def _head_slices(sl):
    return [slice(c, c + HEAD_DIM) for c in range(sl.start, sl.stop, HEAD_DIM)]


def _per_sub_tile(tn, matmuls, epilogue):
    for sl in _col_chunks(tn):
        epilogue(sl, *matmuls(sl))


def _gmlp_in_kernel(x_ref, g_ref, w_ref, b_ref, o_ref, xn_ref):
    @pl.when(pl.program_id(1) == 0)
    def _():
        _store_normed(x_ref, g_ref, xn_ref)

    def finish(sl, acc):
        o_ref[:, sl] = _gelu_tanh(acc + b_ref[:, sl]).astype(o_ref.dtype)

    _per_sub_tile(o_ref.shape[1], lambda sl: (_dot(xn_ref[...], w_ref[:, sl]),), finish)


def _fox_in_kernel(x_ref, g_ref, w_ref, gain_ref, w2_ref, o_ref, o2_ref, xn_ref, *, n_norm_tiles):
    j = pl.program_id(1)

    @pl.when(j == 0)
    def _():
        _store_normed(x_ref, g_ref, xn_ref)
        o2_ref[...] = _dot(xn_ref[...], w2_ref[...])

    def matmul(sl):
        return (_dot(xn_ref[...], w_ref[:, sl]),)

    def head_norm(sl, acc):
        for k, hs in enumerate(_head_slices(sl)):
            blk = acc[:, k * HEAD_DIM:(k + 1) * HEAD_DIM]
            o_ref[:, hs] = _rms_rows(blk, gain_ref[:, hs]).astype(o_ref.dtype)

    def plain(sl, acc):
        o_ref[:, sl] = acc.astype(o_ref.dtype)

    @pl.when(j < n_norm_tiles)
    def _():
        _per_sub_tile(o_ref.shape[1], matmul, head_norm)

    @pl.when(j >= n_norm_tiles)
    def _():
        _per_sub_tile(o_ref.shape[1], matmul, plain)


def _seq_start_mask(prev, tiles_per_seq):
    first = (pl.program_id(0) % tiles_per_seq) == 0
    return jnp.where(first, 0.0, prev)


def _gdn_in_kernel(x_ref, xp_ref, g_ref, w_ref, cw_ref, scale_ref, w2_ref, o_ref, o2_ref, xn_ref, xnp_ref,
                   rows_ref, *, n_qk_tiles, n_conv_tiles, tiles_per_seq):
    j = pl.program_id(1)

    @pl.when(j == 0)
    def _():
        _store_normed(x_ref, g_ref, xn_ref)
        _store_normed(xp_ref, g_ref, xnp_ref)
        o2_ref[...] = _dot(xn_ref[...], w2_ref[...])

    def matmul(sl):
        return (_dot(xn_ref[...], w_ref[:, sl]),)

    def matmul_with_halo(sl):
        return matmul(sl) + (_seq_start_mask(_dot(xnp_ref[...], w_ref[:, sl]), tiles_per_seq),)

    def conv_silu_l2norm(sl, acc, prev):
        y = _silu(_causal_conv_rows(acc, prev, cw_ref[:, sl], rows_ref))
        for k, hs in enumerate(_head_slices(sl)):
            blk = y[:, k * HEAD_DIM:(k + 1) * HEAD_DIM]
            inv = lax.rsqrt(jnp.sum(blk * blk, axis=-1, keepdims=True) + RMS_EPS)
            o_ref[:, hs] = (blk * inv * scale_ref[:, hs]).astype(o_ref.dtype)

    def conv_silu(sl, acc, prev):
        o_ref[:, sl] = _silu(_causal_conv_rows(acc, prev, cw_ref[:, sl], rows_ref)).astype(o_ref.dtype)

    def plain(sl, acc):
        o_ref[:, sl] = acc.astype(o_ref.dtype)

    @pl.when(j < n_qk_tiles)
    def _():
        _per_sub_tile(o_ref.shape[1], matmul_with_halo, conv_silu_l2norm)

    @pl.when(jnp.logical_and(j >= n_qk_tiles, j < n_conv_tiles))
    def _():
        _per_sub_tile(o_ref.shape[1], matmul_with_halo, conv_silu)

    @pl.when(j >= n_conv_tiles)
    def _():
        _per_sub_tile(o_ref.shape[1], matmul, plain)


def _ffn_in_kernel(x_ref, xp_ref, g_ref, wg_ref, wu_ref, cw_ref, cb_ref, o_ref, xn_ref, xnp_ref, *, tiles_per_seq):
    @pl.when(pl.program_id(1) == 0)
    def _():
        _store_normed(x_ref, g_ref, xn_ref)
        _store_normed(xp_ref, g_ref, xnp_ref)

    for sl in _col_chunks(o_ref.shape[1]):
        gate = _dot(xn_ref[...], wg_ref[:, sl])
        up = _dot(xn_ref[...], wu_ref[:, sl])
        prev = _seq_start_mask(_dot(xnp_ref[...], wg_ref[:, sl]), tiles_per_seq)
        gate = _causal_conv_rows(gate, prev, cw_ref[:, sl]) + cb_ref[:, sl]
        o_ref[:, sl] = (_silu(gate) * up).astype(o_ref.dtype)


def _row_tile(t):
    return min(TILE_M, t)


def _col_tile(n, align, limit=TILE_N):
    tn = min(limit, n)
    while n % tn or align % tn:
        tn -= LANES
    return tn


def _x_specs(tm, d, halo):
    specs = [pl.BlockSpec((tm, d), lambda i, j: (i, 0))]
    if halo:
        per = tm // HALO
        specs.append(pl.BlockSpec((HALO, d), lambda i, j: (jnp.maximum(i * per - 1, 0), 0)))
    return specs


def _col(rows, tn):
    return pl.BlockSpec((rows, tn), lambda i, j: (0, j))


def _conv_rows_scratch(tm, tn):
    chunk = _col_chunks(tn)[0]
    return pltpu.VMEM((SUBLANES + tm, chunk.stop - chunk.start), F32)


def _gmlp_in(x, gain, w, b):
    t, d = x.shape
    n = w.shape[1]
    tm, tn = _row_tile(t), _col_tile(n, n)
    return pl.pallas_call(
        _gmlp_in_kernel,
        grid=(t // tm, n // tn),
        in_specs=_x_specs(tm, d, False) + [pl.BlockSpec((1, d), lambda i, j: (0, 0)), _col(d, tn), _col(1, tn)],
        out_specs=pl.BlockSpec((tm, tn), lambda i, j: (i, j)),
        out_shape=jax.ShapeDtypeStruct((t, n), BF16),
        scratch_shapes=[pltpu.VMEM((tm, d), BF16)],
        compiler_params=_params("parallel", "arbitrary"),
        name="gmlp_in",
    )(x, gain, w, b)


def _fox_in(x, gain, w, head_gain, w2, n, n_norm_cols):
    t, d = x.shape
    tm, tn = _row_tile(t), _col_tile(n, n_norm_cols)
    kern = functools.partial(_fox_in_kernel, n_norm_tiles=n_norm_cols // tn)
    return pl.pallas_call(
        kern,
        grid=(t // tm, n // tn),
        in_specs=_x_specs(tm, d, False) + [pl.BlockSpec((1, d), lambda i, j: (0, 0)), _col(d, tn), _col(1, tn),
                                           pl.BlockSpec((d, LANES), lambda i, j: (0, 0))],
        out_specs=[pl.BlockSpec((tm, tn), lambda i, j: (i, j)), pl.BlockSpec((tm, LANES), lambda i, j: (i, 0))],
        out_shape=[jax.ShapeDtypeStruct((t, n), BF16), jax.ShapeDtypeStruct((t, LANES), F32)],
        scratch_shapes=[pltpu.VMEM((tm, d), BF16)],
        compiler_params=_params("parallel", "arbitrary"),
        name="fox_in",
    )(x, gain, w, head_gain, w2)


def _gdn_in(x, gain, w, conv_w, scale_row, w2, n, n_qk_cols, n_conv_cols, seq):
    t, d = x.shape
    tm, tn = _row_tile(min(t, seq)), _col_tile(n, n_qk_cols)
    kern = functools.partial(_gdn_in_kernel, n_qk_tiles=n_qk_cols // tn, n_conv_tiles=n_conv_cols // tn,
                             tiles_per_seq=seq // tm)
    return pl.pallas_call(
        kern,
        grid=(t // tm, n // tn),
        in_specs=_x_specs(tm, d, True) + [pl.BlockSpec((1, d), lambda i, j: (0, 0)), _col(d, tn),
                                          _col(conv_w.shape[0], tn), _col(1, tn),
                                          pl.BlockSpec((d, LANES), lambda i, j: (0, 0))],
        out_specs=[pl.BlockSpec((tm, tn), lambda i, j: (i, j)), pl.BlockSpec((tm, LANES), lambda i, j: (i, 0))],
        out_shape=[jax.ShapeDtypeStruct((t, n), BF16), jax.ShapeDtypeStruct((t, LANES), F32)],
        scratch_shapes=[pltpu.VMEM((tm, d), BF16), pltpu.VMEM((HALO, d), BF16), _conv_rows_scratch(tm, tn)],
        compiler_params=_params("parallel", "arbitrary"),
        name="gdn_in",
    )(x, x, gain, w, conv_w, scale_row, w2)


def _ffn_in(x, gain, wg, wu, conv_w, conv_b, seq):
    t, d = x.shape
    n = wg.shape[1]
    tm, tn = _row_tile(min(t, seq)), _col_tile(n, n, SUB_TILE_N)
    kern = functools.partial(_ffn_in_kernel, tiles_per_seq=seq // tm)
    return pl.pallas_call(
        kern,
        grid=(t // tm, n // tn),
        in_specs=_x_specs(tm, d, True) + [pl.BlockSpec((1, d), lambda i, j: (0, 0)), _col(d, tn), _col(d, tn),
                                          _col(conv_w.shape[0], tn), _col(1, tn)],
        out_specs=pl.BlockSpec((tm, tn), lambda i, j: (i, j)),
        out_shape=jax.ShapeDtypeStruct((t, n), BF16),
        scratch_shapes=[pltpu.VMEM((tm, d), BF16), pltpu.VMEM((HALO, d), BF16)],
        compiler_params=_params("parallel", "arbitrary"),
        name="ffn_in",
    )(x, x, gain, wg, wu, conv_w, conv_b)


def _out_proj_kernel(a_ref, w_ref, x_ref, o_ref):
    for sl in _col_chunks(o_ref.shape[1]):
        o_ref[:, sl] = x_ref[:, sl] + _dot(a_ref[...], w_ref[:, sl])


def _out_proj(a, w, x):
    t, k = a.shape
    d = w.shape[1]
    resident = k * d * w.dtype.itemsize <= RESIDENT_WEIGHT_BYTES
    tm, tn = _row_tile(t), d if resident else min(OUT_TILE_N, d)
    w_spec = (pl.BlockSpec((k, tn), lambda i, j: (0, 0), pipeline_mode=pl.Buffered(1)) if resident
              else pl.BlockSpec((k, tn), lambda i, j: (0, j)))
    return pl.pallas_call(
        _out_proj_kernel,
        grid=(t // tm, d // tn),
        in_specs=[pl.BlockSpec((tm, k), lambda i, j: (i, 0)), w_spec,
                  pl.BlockSpec((tm, tn), lambda i, j: (i, j))],
        out_specs=pl.BlockSpec((tm, tn), lambda i, j: (i, j)),
        out_shape=jax.ShapeDtypeStruct((t, d), F32),
        compiler_params=_params("parallel", "arbitrary"),
        name="out_proj",
    )(a, w, x)


def _gmlp_gate_kernel(u_ref, v_ref, vg_ref, ws_ref, bst_ref, o_ref, vn_ref):
    vn_ref[...] = _rms_rows(v_ref[...].astype(F32), vg_ref[...]).astype(BF16)
    tm = u_ref.shape[0]
    groups = ws_ref.shape[0]
    row = lax.broadcasted_iota(jnp.int32, (GMLP_CHUNK, GMLP_CHUNK), 0)
    col = lax.broadcasted_iota(jnp.int32, (GMLP_CHUNK, GMLP_CHUNK), 1)
    for g in range(groups):
        cs = slice(g * HEAD_DIM, (g + 1) * HEAD_DIM)
        w_causal = jnp.where(row >= col, ws_ref[g], 0.0).astype(BF16)
        bias = bst_ref[:, g:g + 1]
        for c in range(tm // GMLP_CHUNK):
            rs = slice(c * GMLP_CHUNK, (c + 1) * GMLP_CHUNK)
            sv = _dot(w_causal, vn_ref[rs, cs]) + bias
            o_ref[rs, cs] = (u_ref[rs, cs].astype(F32) * sv).astype(o_ref.dtype)


def _gmlp_gate(h, v_gain, w_s, b_s_t):
    t, two_a = h.shape
    a = two_a // 2
    tm = min(GMLP_TILE_M, t)
    groups = w_s.shape[0]
    return pl.pallas_call(
        _gmlp_gate_kernel,
        grid=(t // tm,),
        in_specs=[pl.BlockSpec((tm, a), lambda i: (i, 0)), pl.BlockSpec((tm, a), lambda i: (i, 1)),
                  pl.BlockSpec((1, a), lambda i: (0, 0)),
                  pl.BlockSpec((groups, GMLP_CHUNK, GMLP_CHUNK), lambda i: (0, 0, 0)),
                  pl.BlockSpec((GMLP_CHUNK, groups), lambda i: (0, 0))],
        out_specs=pl.BlockSpec((tm, a), lambda i: (i, 0)),
        out_shape=jax.ShapeDtypeStruct((t, a), BF16),
        scratch_shapes=[pltpu.VMEM((tm, a), BF16)],
        compiler_params=_params("parallel"),
        name="gmlp_gate",
    )(h, h, v_gain, w_s, b_s_t)


def _tri_ones(n):
    row = lax.broadcasted_iota(jnp.int32, (n, n), 0)
    col = lax.broadcasted_iota(jnp.int32, (n, n), 1)
    return jnp.where(row >= col, 1.0, 0.0).astype(BF16)


def _cumsum_rows(tri, x):
    hi, mid, lo = _split3(x)
    return _dot(tri, hi) + _dot(tri, mid) + _dot(tri, lo)


def _fox_decay_kernel(fl_ref, bf_ref, ccol_ref, crow_ref):
    tk = crow_ref.shape[2]
    tri = _tri_ones(tk)
    carry = jnp.zeros((1, LANES), F32)
    for n in range(crow_ref.shape[0]):
        rs = slice(n * tk, (n + 1) * tk)
        log_f = -_softplus(-(fl_ref[rs, :] + bf_ref[...]))
        c = _cumsum_rows(tri, log_f) + carry
        carry = c[tk - 1:tk, :]
        ccol_ref[rs, :] = c
        crow_ref[n] = c.T


def _fox_decay(fl, b_f_row, batch, seq, tk):
    nkv = seq // tk
    return pl.pallas_call(
        _fox_decay_kernel,
        grid=(batch,),
        in_specs=[pl.BlockSpec((seq, LANES), lambda b: (b, 0)), pl.BlockSpec((1, LANES), lambda b: (0, 0))],
        out_specs=[pl.BlockSpec((seq, LANES), lambda b: (b, 0)),
                   pl.BlockSpec((None, nkv, LANES, tk), lambda b: (b, 0, 0, 0))],
        out_shape=[jax.ShapeDtypeStruct((batch * seq, LANES), F32),
                   jax.ShapeDtypeStruct((batch, nkv, LANES, tk), F32)],
        compiler_params=_params("parallel"),
        name="fox_decay",
    )(fl, b_f_row)


def _attn_chains(nq, n_chains=2):
    chains = [[] for _ in range(n_chains)]
    for i in reversed(range(nq)):
        min(chains, key=len).extend((i, j) for j in range(i + 1))
    return chains


def _fox_attn_kernel(q_ref, k_ref, v_ref, og_ref, ccol_ref, crow_ref, o_ref, *, tile, scale):
    h = pl.program_id(1)
    nq = q_ref.shape[0] // tile
    log2e = 1.4426950408889634
    row_in_tile = h % SUBLANES
    row = lax.broadcasted_iota(jnp.int32, (tile, tile), 0)
    col = lax.broadcasted_iota(jnp.int32, (tile, tile), 1)
    chains = _attn_chains(nq)
    state = [None] * len(chains)

    def rows(i):
        return pl.ds(i * tile, tile)

    for step in range(max(len(c) for c in chains)):
        live = [(n, c[step]) for n, c in enumerate(chains) if step < len(c)]
        scores = [_dot_nt(q_ref[rows(i), :], k_ref[rows(j), :]) for _, (i, j) in live]
        probs = []
        for (n, (i, j)), s in zip(live, scores):
            c_t = _lane_select(ccol_ref[rows(i), :], h) * log2e
            c_s = crow_ref[j, pl.ds(row_in_tile, 1), :] * log2e
            t = s * (scale * log2e) - c_s
            if i == j:
                t = jnp.where(row >= col, t, MASKED)
            m_new = jnp.max(t, axis=1, keepdims=True) + c_t
            if j > 0:
                m_old, l_old, acc_old = state[n]
                m_new = jnp.maximum(m_old, m_new)
            p = jnp.exp2(t - (m_new - c_t))
            l_new = jnp.sum(p, axis=1, keepdims=True)
            if j > 0:
                alpha = jnp.exp2(m_old - m_new)
                l_new = alpha * l_old + l_new
                state[n] = (m_new, l_new, alpha * acc_old)
            else:
                state[n] = (m_new, l_new, None)
            probs.append(p.astype(BF16))
        updates = [_dot(p, v_ref[rows(j), :]) for (_, (_, j)), p in zip(live, probs)]
        for (n, (i, j)), pv in zip(live, updates):
            m, l, acc = state[n]
            acc = pv if acc is None else acc + pv
            state[n] = (m, l, acc)
            if i == j:
                gate = _sigmoid(og_ref[rows(i), :].astype(F32))
                o_ref[rows(i), :] = (acc / l * gate).astype(o_ref.dtype)


def _fox_attn(proj, ccol, crow, batch, seq, heads):
    t = proj.shape[0]
    tile = crow.shape[3]
    nq = seq // tile
    kern = functools.partial(_fox_attn_kernel, tile=tile, scale=HEAD_DIM ** -0.5)

    def head_block(offset):
        return pl.BlockSpec((seq, HEAD_DIM), lambda b, h: (b, offset + h))

    return pl.pallas_call(
        kern,
        grid=(batch, heads),
        in_specs=[head_block(0), head_block(heads), head_block(2 * heads), head_block(3 * heads),
                  pl.BlockSpec((seq, LANES), lambda b, h: (b, 0)),
                  pl.BlockSpec((None, nq, SUBLANES, tile), lambda b, h: (b, 0, h // SUBLANES, 0))],
        out_specs=head_block(0),
        out_shape=jax.ShapeDtypeStruct((t, heads * HEAD_DIM), BF16),
        compiler_params=_params("parallel", "arbitrary"),
        name="fox_attn",
    )(proj, proj, proj, proj, ccol, crow)


def _gdn_gates_kernel(ba_ref, alog_ref, dtb_ref, gcol_ref, grow_ref):
    chunk = grow_ref.shape[2]
    tri = _tri_ones(chunk)
    lane = lax.broadcasted_iota(jnp.int32, (chunk, LANES), 1)
    for n in range(grow_ref.shape[0]):
        rs = slice(n * chunk, (n + 1) * chunk)
        raw = ba_ref[rs, :]
        beta = _sigmoid(raw)
        g = -jnp.exp(alog_ref[...]) * _softplus(raw + dtb_ref[...])
        packed = jnp.where(lane < GATE_LANE, beta, _cumsum_rows(tri, g))
        gcol_ref[rs, :] = packed
        grow_ref[n] = packed.T


def _gdn_gates(ba, alog_row, dtb_row, batch, seq):
    nchunks = seq // GDN_CHUNK
    return pl.pallas_call(
        _gdn_gates_kernel,
        grid=(batch,),
        in_specs=[pl.BlockSpec((seq, LANES), lambda b: (b, 0)), pl.BlockSpec((1, LANES), lambda b: (0, 0)),
                  pl.BlockSpec((1, LANES), lambda b: (0, 0))],
        out_specs=[pl.BlockSpec((seq, LANES), lambda b: (b, 0)),
                   pl.BlockSpec((None, nchunks, LANES, GDN_CHUNK), lambda b: (b, 0, 0, 0))],
        out_shape=[jax.ShapeDtypeStruct((batch * seq, LANES), F32),
                   jax.ShapeDtypeStruct((batch, nchunks, LANES, GDN_CHUNK), F32)],
        compiler_params=_params("parallel"),
        name="gdn_gates",
    )(ba, alog_row, dtb_row)


def _block_diag(pack):
    n = pack.shape[0]
    reps = pack.shape[1] // n
    zero = jnp.zeros((n, n), pack.dtype)
    rows = []
    for r in range(reps):
        blk = pack[:, r * n:(r + 1) * n]
        rows.append(jnp.concatenate([blk if c == r else zero for c in range(reps)], axis=1))
    return jnp.concatenate(rows, axis=0)


def _unit_lower_inverses(mats):
    n = mats[0].shape[0]
    row = lax.broadcasted_iota(jnp.int32, mats[0].shape, 0)
    col = lax.broadcasted_iota(jnp.int32, mats[0].shape, 1)
    eye = jnp.where(row == col % n, 1.0, 0.0)
    bs = [-a for a in mats]
    b_his = [b.astype(BF16) for b in bs]
    ps = [eye + b for b in bs]
    ms = [_dot(b_hi, _block_diag(b_hi)) for b_hi in b_his]
    power = 2
    while power < n:
        mbs = [m.astype(BF16) for m in ms]
        if 2 * power < n:
            both = [_dot(jnp.concatenate([p.astype(BF16), mb], axis=0), _block_diag(mb)) for p, mb in zip(ps, mbs)]
            ps = [p + x[:n] for p, x in zip(ps, both)]
            ms = [x[n:] for x in both]
        else:
            ps = [p + _dot(p.astype(BF16), _block_diag(mb)) for p, mb in zip(ps, mbs)]
        power *= 2
    b_los = [(b - b_hi.astype(F32)).astype(BF16) for b, b_hi in zip(bs, b_his)]
    p_his = [p.astype(BF16) for p in ps]
    p_los = [(p - p_hi.astype(F32)).astype(BF16) for p, p_hi in zip(ps, p_his)]
    bps = [_dot(jnp.concatenate([b_hi, b_lo], axis=0), _block_diag(p_hi))
           for b_hi, b_lo, p_hi in zip(b_his, b_los, p_his)]
    cross = [_dot(b_hi, _block_diag(p_lo)) for b_hi, p_lo in zip(b_his, p_los)]
    resids = [(eye - p) + (bp[:n] + (bp[n:] + x)) for p, bp, x in zip(ps, bps, cross)]
    return [p + _dot(p_hi, _block_diag(r.astype(BF16))) for p, p_hi, r in zip(ps, p_his, resids)]


def _gdn_kernel(q_ref, k_ref, v_ref, z_ref, gcol_ref, grow_ref, onorm_ref, o_ref,
                u_ref, w_ref, attn_ref, state_ref, *, k_per_step, rep, chunks_per_iter):
    step = pl.program_id(1)
    nchunks, chunk, _ = q_ref.shape
    row = lax.broadcasted_iota(jnp.int32, (chunk, chunk), 0)
    col = lax.broadcasted_iota(jnp.int32, (chunk, chunk), 1)

    def head_slice(idx):
        return slice(idx * HEAD_DIM, (idx + 1) * HEAD_DIM)

    def pass1(i, _):
        jobs = [(i * chunks_per_iter + c, j) for c in range(chunks_per_iter) for j in range(k_per_step)]
        qs = [q_ref[n, :, head_slice(j)] for n, j in jobs]
        ks = [k_ref[n, :, head_slice(j)] for n, j in jobs]
        kks = [_dot_nt(k, k) for k in ks]
        qks = [_dot_nt(q, k) for q, k in zip(qs, ks)]
        a_packs, rhs = [], []
        for (n, j), k, kk, qk in zip(jobs, ks, kks, qks):
            gates = gcol_ref[n]
            kf = k.astype(F32)
            a_blocks, attn_blocks, job_rhs = [], [], []
            for r in range(rep):
                hv = (step * k_per_step + j) * rep + r
                beta = _lane_select(gates, hv)
                gc = _lane_select(gates, GATE_LANE + hv)
                gc_row = grow_ref[n, pl.ds(GATE_LANE + hv, 1), :]
                decay = jnp.exp(jnp.where(row >= col, gc - gc_row, MASKED))
                a_blocks.append(jnp.where(row > col, beta * kk * decay, 0.0))
                attn_blocks.append((qk * decay).astype(BF16))
                v = v_ref[n, :, head_slice(j * rep + r)].astype(F32)
                job_rhs.append(jnp.concatenate([(v * beta).astype(BF16),
                                                (kf * (beta * jnp.exp(gc))).astype(BF16)], axis=1))
            attn_ref[n, j] = jnp.concatenate(attn_blocks, axis=1)
            a_packs.append(jnp.concatenate(a_blocks, axis=1))
            rhs.append(job_rhs)
        t_invs = [t.astype(BF16) for t in _unit_lower_inverses(a_packs)]
        sols = [[_dot(t_inv[:, head_slice(r)], job_rhs[r]) for r in range(rep)]
                for t_inv, job_rhs in zip(t_invs, rhs)]
        for (n, j), job_sols in zip(jobs, sols):
            u_ref[n, j] = jnp.concatenate([s[:, :HEAD_DIM] for s in job_sols], axis=1)
            w_ref[n, j] = jnp.concatenate([s[:, HEAD_DIM:] for s in job_sols], axis=1).astype(BF16)
        return 0

    def pass2(n, _):
        gates = gcol_ref[n]
        heads = range(k_per_step)
        qfs = [q_ref[n, :, head_slice(j)].astype(F32) for j in heads]
        kfs = [k_ref[n, :, head_slice(j)].astype(F32) for j in heads]
        gcs = [[_lane_select(gates, GATE_LANE + (step * k_per_step + j) * rep + r) for r in range(rep)] for j in heads]
        states = [[state_ref[j, r] for r in range(rep)] for j in heads]
        q_decs = [jnp.concatenate([(qfs[j] * jnp.exp(gc)).astype(BF16) for gc in gcs[j]], axis=1) for j in heads]
        s_bds = [_block_diag(jnp.concatenate([s.astype(BF16) for s in states[j]], axis=1)) for j in heads]
        boths = [_dot(jnp.concatenate([w_ref[n, j], q_decs[j]], axis=0), s_bds[j]) for j in heads]
        vbs = [(u_ref[n, j] - boths[j][:chunk]).astype(BF16) for j in heads]
        g_lasts = [[gc[chunk - 1:chunk, :] for gc in gcs[j]] for j in heads]
        k_decs = [[(kfs[j] * jnp.exp(g_lasts[j][r] - gcs[j][r])).astype(BF16) for r in range(rep)] for j in heads]
        updates = [[_dot_tn(k_decs[j][r], vbs[j][:, head_slice(r)]) for r in range(rep)] for j in heads]
        for j in heads:
            for r in range(rep):
                state_ref[j, r] = states[j][r] * jnp.exp(g_lasts[j][r]) + updates[j][r]
        outs = [boths[j][chunk:] + _dot(attn_ref[n, j], _block_diag(vbs[j])) for j in heads]
        for j in heads:
            for r in range(rep):
                hs = head_slice(j * rep + r)
                zf = z_ref[n, :, hs].astype(F32)
                o_ref[n, :, hs] = (_rms_rows(outs[j][:, head_slice(r)], onorm_ref[...]) * _silu(zf)).astype(o_ref.dtype)
        return 0

    lax.fori_loop(0, nchunks // chunks_per_iter, pass1, 0)

    @pl.when(pl.program_id(2) == 0)
    def _():
        state_ref[...] = jnp.zeros_like(state_ref)

    lax.fori_loop(0, nchunks, pass2, 0)


def _gdn(proj, gcol, grow, out_norm, batch, seq, k_heads, v_heads):
    t = proj.shape[0]
    nchunks = seq // GDN_CHUNK
    rep = v_heads // k_heads
    kps = min(GDN_K_HEADS_PER_STEP, k_heads)
    steps = k_heads // kps
    parts = GDN_SEQ_PARTS if nchunks % GDN_SEQ_PARTS == 0 else 1
    nc = nchunks // parts
    proj3 = proj.reshape(batch * nchunks, GDN_CHUNK, proj.shape[1])
    gcol3 = gcol.reshape(batch * nchunks, GDN_CHUNK, LANES)
    kw, vw = kps * HEAD_DIM, kps * rep * HEAD_DIM
    v_base = 2 * k_heads * HEAD_DIM // vw
    z_base = v_base + v_heads * HEAD_DIM // vw
    kern = functools.partial(_gdn_kernel, k_per_step=kps, rep=rep, chunks_per_iter=min(GDN_CHUNKS_PER_ITER, nc))
    pack = rep * HEAD_DIM

    def cols(width, offset):
        return pl.BlockSpec((nc, GDN_CHUNK, width), lambda b, h, s: (b * parts + s, 0, offset + h))

    out = pl.pallas_call(
        kern,
        grid=(batch, steps, parts),
        in_specs=[cols(kw, 0), cols(kw, steps), cols(vw, v_base), cols(vw, z_base),
                  pl.BlockSpec((nc, GDN_CHUNK, LANES), lambda b, h, s: (b * parts + s, 0, 0)),
                  pl.BlockSpec((None, nc, LANES, GDN_CHUNK), lambda b, h, s: (b, s, 0, 0)),
                  pl.BlockSpec((1, HEAD_DIM), lambda b, h, s: (0, 0))],
        out_specs=cols(vw, 0),
        out_shape=jax.ShapeDtypeStruct((batch * nchunks, GDN_CHUNK, v_heads * HEAD_DIM), BF16),
        scratch_shapes=[pltpu.VMEM((nc, kps, GDN_CHUNK, pack), F32),
                        pltpu.VMEM((nc, kps, GDN_CHUNK, pack), BF16),
                        pltpu.VMEM((nc, kps, GDN_CHUNK, pack), BF16),
                        pltpu.VMEM((kps, rep, HEAD_DIM, HEAD_DIM), F32)],
        compiler_params=_params("parallel", "arbitrary", "arbitrary"),
        name="gdn_delta_rule",
    )(proj3, proj3, proj3, proj3, gcol3, grow, out_norm)
    return out.reshape(t, v_heads * HEAD_DIM)


def _row(v, width=None):
    v = v.reshape(1, -1).astype(F32)
    if width is not None and v.shape[1] < width:
        v = jnp.pad(v, ((0, 0), (0, width - v.shape[1])))
    return v


def _mixer_gmlp(x, gain, w_in, b_in, v_norm, w_s, b_s, w_out):
    h = _gmlp_in(x, _row(gain), w_in.astype(BF16), _row(b_in))
    gated = _gmlp_gate(h, _row(v_norm), w_s, b_s.T)
    return _out_proj(gated, w_out.astype(BF16), x)


def _mixer_fox(x, gain, w_in, b_f, q_norm, k_norm, w_out, batch, seq):
    heads = b_f.shape[0]
    dim = heads * HEAD_DIM
    w_f = jnp.pad(w_in[:, 4 * dim:], ((0, 0), (0, LANES - heads))).astype(BF16)
    head_gain = jnp.concatenate([jnp.tile(q_norm, heads), jnp.tile(k_norm, heads), jnp.ones((2 * dim,), F32)])
    proj, fl = _fox_in(x, _row(gain), w_in.astype(BF16), _row(head_gain), w_f, 4 * dim, 2 * dim)
    tq = min(ATTN_TILE, seq)
    ccol, crow = _fox_decay(fl, _row(b_f, LANES), batch, seq, tq)
    o = _fox_attn(proj, ccol, crow, batch, seq, heads)
    return _out_proj(o, w_out.astype(BF16), x)


def _mixer_gdn(x, gain, w_in, conv_w, a_log, dt_bias, out_norm, w_out, batch, seq):
    v_heads = a_log.shape[0]
    dv = v_heads * HEAD_DIM
    n_qkv = conv_w.shape[1]
    dk = (n_qkv - dv) // 2
    k_heads = dk // HEAD_DIM
    assert v_heads <= GATE_LANE
    w_b = w_in[:, n_qkv + dv:n_qkv + dv + v_heads]
    w_a = w_in[:, n_qkv + dv + v_heads:]
    pad = GATE_LANE - v_heads
    w_ba = jnp.pad(jnp.concatenate([jnp.pad(w_b, ((0, 0), (0, pad))), w_a], axis=1),
                   ((0, 0), (0, LANES - GATE_LANE - v_heads))).astype(BF16)
    conv_full = jnp.pad(conv_w, ((0, 0), (0, dv)))
    scale_row = jnp.concatenate([jnp.full((dk,), HEAD_DIM ** -0.5, F32), jnp.ones((dk + 2 * dv,), F32)])
    proj, ba = _gdn_in(x, _row(gain), w_in.astype(BF16), conv_full, _row(scale_row), w_ba, n_qkv + dv, 2 * dk, n_qkv, seq)
    alog_row = jnp.pad(_row(a_log), ((0, 0), (GATE_LANE, LANES - GATE_LANE - v_heads)))
    dtb_row = jnp.pad(_row(dt_bias), ((0, 0), (GATE_LANE, LANES - GATE_LANE - v_heads)))
    gcol, grow = _gdn_gates(ba, alog_row, dtb_row, batch, seq)
    o = _gdn(proj, gcol, grow, _row(out_norm), batch, seq, k_heads, v_heads)
    return _out_proj(o, w_out.astype(BF16), x)


def _conv_ffn(x, gain, w_gate, w_up, conv_w, conv_b, w_down, seq):
    h = _ffn_in(x, _row(gain), w_gate.astype(BF16), w_up.astype(BF16), conv_w, _row(conv_b), seq)
    return _out_proj(h, w_down.astype(BF16), x)


def kernel(x, norm_mix, norm_ffn, ffn_w_gate, ffn_w_up, ffn_conv_w, ffn_conv_b, ffn_w_down, a_w_in, a_b_in, a_v_norm, a_w_s, a_b_s, a_w_out, b_w_in, b_b_f, b_q_norm, b_k_norm, b_w_out, c_w_in, c_conv_w, c_a_log, c_dt_bias, c_out_norm, c_w_out):
    batch, seq, d = x.shape
    depth = norm_mix.shape[0]
    xt = x.reshape(batch * seq, d)
    for i in range(depth):
        kind, j = i % 3, i // 3
        if kind == 0:
            xt = _mixer_gmlp(xt, norm_mix[i], a_w_in[j], a_b_in[j], a_v_norm[j], a_w_s[j], a_b_s[j], a_w_out[j])
        elif kind == 1:
            xt = _mixer_fox(xt, norm_mix[i], b_w_in[j], b_b_f[j], b_q_norm[j], b_k_norm[j], b_w_out[j], batch, seq)
        else:
            xt = _mixer_gdn(xt, norm_mix[i], c_w_in[j], c_conv_w[j], c_a_log[j], c_dt_bias[j], c_out_norm[j],
                            c_w_out[j], batch, seq)
        xt = _conv_ffn(xt, norm_ffn[i], ffn_w_gate[i], ffn_w_up[i], ffn_conv_w[i], ffn_conv_b[i], ffn_w_down[i], seq)
    return xt.reshape(batch, seq, d)
```
